```python
import math
import jax, jax.numpy as jnp
from jax import lax
import numpy as np

D_MODEL = 1024
BATCH = 8
SEQ = 2048
DEPTH = 2
DEC_BATCH = 32
DEC_SEQ = 1
PAST_LEN = 16384
PAGE_SIZE = 128

SSM_EXPAND = 2
D_INNER = SSM_EXPAND * D_MODEL
SSM_HEAD_DIM = 64
SSM_HEADS = D_INNER // SSM_HEAD_DIM
SSM_GROUPS = 4
SSM_HPG = SSM_HEADS // SSM_GROUPS
D_STATE = 128
CONV_W = 4
CONV_DIM = D_INNER + 2 * SSM_GROUPS * D_STATE
SSD_CHUNK = 128

ATT_HEADS = 16
KV_HEADS = 4
HEAD_DIM = 64
ATT_REP = ATT_HEADS // KV_HEADS
ATT_WIDTH = ATT_HEADS * HEAD_DIM
KV_WIDTH = KV_HEADS * HEAD_DIM
IDX_HEADS = 8
IDX_DIM = 64
IDX_SCALE = (IDX_HEADS * IDX_DIM) ** -0.5
TOPK_MAX = 256
Q_BLOCK = 128
ROPE_THETA = 10000.0

PEER_HEADS = 8
N_KEYS = 128
N_EXPERTS = N_KEYS * N_KEYS
PEER_KEY_DIM = 256
PEER_HALF = PEER_KEY_DIM // 2
PEER_TOPK = 16
PEER_BLOCK = 128

EPS = 1e-6

IN_SIZES = (D_INNER, CONV_DIM, SSM_HEADS,
            ATT_WIDTH, KV_WIDTH, KV_WIDTH,
            IDX_HEADS * IDX_DIM, IDX_DIM, IDX_HEADS,
            D_MODEL, D_MODEL)
IN_WIDTH = sum(IN_SIZES)

kernel_name = 'hybrid_ssd_dsa_peer_step'


def _in_offsets():
    return [int(o) for o in np.cumsum(IN_SIZES)[:-1]]


def rms_norm(x, g):
    xf = x.astype(jnp.float32)
    y = xf * lax.rsqrt(jnp.mean(xf * xf, axis=-1, keepdims=True) + EPS)
    return (y * g.astype(jnp.float32)).astype(x.dtype)


def rope(x, pos):
    d = x.shape[-1]
    inv = ROPE_THETA ** (-jnp.arange(0, d, 2, dtype=jnp.float32) / d)
    ang = pos.astype(jnp.float32)[:, None] * inv[None, :]
    ang = ang.reshape((ang.shape[0],) + (1,) * (x.ndim - 3) + (d // 2,))
    cos, sin = jnp.cos(ang), jnp.sin(ang)
    xf = x.astype(jnp.float32)
    x1, x2 = xf[..., :d // 2], xf[..., d // 2:]
    return jnp.concatenate([x1 * cos - x2 * sin, x2 * cos + x1 * sin], axis=-1).astype(x.dtype)


def _gather_rows(rows, idx):
    return jax.vmap(lambda r, i: r[i])(rows, idx)


def ssd_chunked(xs, dt, a, bm, cm, h0):
    bsz, seq = xs.shape[:2]
    nc = seq // SSD_CHUNK

    def to_chunks(t):
        return jnp.moveaxis(t.reshape((bsz, nc, SSD_CHUNK) + t.shape[2:]), 1, 0)

    xd = xs * dt[..., None]
    ad = dt * a
    causal = jnp.tril(jnp.ones((SSD_CHUNK, SSD_CHUNK), dtype=bool))

    def step(h, inp):
        xd_c, ad_c, b_c, c_c = inp
        acs = jnp.cumsum(ad_c, axis=1)
        seg = acs[:, :, None] - acs[:, None, :]
        lmat = jnp.exp(jnp.where(causal[None, :, :, None, None], seg, -jnp.inf))
        cb = jnp.einsum('blgn,bsgn->blsg', c_c, b_c)
        y_diag = jnp.einsum('blsg,blsgr,bsgrp->blgrp', cb, lmat, xd_c)
        y_off = jnp.einsum('blgn,bgrpn,blgr->blgrp', c_c, h, jnp.exp(acs))
        decay_end = jnp.exp(acs[:, -1:] - acs)
        h_new = h * jnp.exp(acs[:, -1])[..., None, None] + jnp.einsum('bsgn,bsgr,bsgrp->bgrpn', b_c, decay_end, xd_c)
        return h_new, y_diag + y_off

    h, ys = lax.scan(step, h0, (to_chunks(xd), to_chunks(ad), to_chunks(bm), to_chunks(cm)))
    y = jnp.moveaxis(ys, 0, 1).reshape(xs.shape)
    return y, h


def ssd_recurrent(xs, dt, a, bm, cm, h0):
    def step(h, inp):
        x_t, dt_t, b_t, c_t = inp
        h = h * jnp.exp(dt_t * a)[..., None, None] + jnp.einsum('bgrp,bgn->bgrpn', x_t * dt_t[..., None], b_t)
        return h, jnp.einsum('bgn,bgrpn->bgrp', c_t, h)

    tf = lambda t: jnp.moveaxis(t, 1, 0)
    h, ys = lax.scan(step, h0, (tf(xs), tf(dt), tf(bm), tf(cm)))
    return jnp.moveaxis(ys, 0, 1), h


def mamba2_branch(z, xbc, dt_raw, conv_prefix, h0, conv_w, conv_b, dt_bias, a_log, d_skip, norm_g, chunked):
    bsz, t_len, _ = z.shape
    f32 = jnp.float32
    xp = jnp.concatenate([conv_prefix.astype(xbc.dtype), xbc], axis=1)
    conv = conv_b + sum(xp[:, j:j + t_len] * conv_w[j] for j in range(CONV_W))
    conv_new = xp[:, t_len:]
    xbc = jax.nn.silu(conv)
    xs, bm, cm = jnp.split(xbc, [D_INNER, D_INNER + SSM_GROUPS * D_STATE], axis=-1)
    xs = xs.reshape(bsz, t_len, SSM_GROUPS, SSM_HPG, SSM_HEAD_DIM).astype(f32)
    bm = bm.reshape(bsz, t_len, SSM_GROUPS, D_STATE).astype(f32)
    cm = cm.reshape(bsz, t_len, SSM_GROUPS, D_STATE).astype(f32)
    dt = jax.nn.softplus(dt_raw.astype(f32) + dt_bias.astype(f32)).reshape(bsz, t_len, SSM_GROUPS, SSM_HPG)
    a = -jnp.exp(a_log.astype(f32)).reshape(SSM_GROUPS, SSM_HPG)
    scan_fn = ssd_chunked if chunked else ssd_recurrent
    y, h = scan_fn(xs, dt, a, bm, cm, h0)
    y = y + d_skip.astype(f32).reshape(SSM_GROUPS, SSM_HPG, 1) * xs
    y = y.reshape(bsz, t_len, D_INNER) * jax.nn.silu(z.astype(f32))
    y = rms_norm(y.reshape(bsz, t_len, SSM_GROUPS, D_INNER // SSM_GROUPS), norm_g.reshape(SSM_GROUPS, -1))
    return y.reshape(bsz, t_len, D_INNER).astype(z.dtype), h, conv_new


def indexer_scores(iq, ik, iw):
    dots = jax.nn.relu(jnp.einsum('bthd,bsd->bths', iq, ik).astype(jnp.float32))
    return jnp.einsum('bths,bth->bts', dots, iw.astype(jnp.float32) * IDX_SCALE)


def sparse_attend(q, ks, vs, valid):
    bsz, t_len = q.shape[:2]
    qg = q.reshape(bsz, t_len, KV_HEADS, ATT_REP, HEAD_DIM)
    s = jnp.einsum('btgrd,btkgd->btgrk', qg, ks).astype(jnp.float32) * (HEAD_DIM ** -0.5)
    s = jnp.where(valid[:, :, None, None, :], s, -jnp.inf)
    p = jax.nn.softmax(s, axis=-1).astype(vs.dtype)
    o = jnp.einsum('btgrk,btkgd->btgrd', p, vs)
    return o.reshape(bsz, t_len, ATT_WIDTH)


def dsa_prompt(q, k, v, iq, ik, iw):
    bsz, seq = q.shape[:2]
    topk = min(TOPK_MAX, seq // 4)
    nb = seq // Q_BLOCK
    key_pos = jnp.arange(seq)

    def blocks(t):
        return jnp.moveaxis(t.reshape((bsz, nb, Q_BLOCK) + t.shape[2:]), 1, 0)

    def one_block(inp):
        qb, iqb, iwb, qpos = inp
        score = indexer_scores(iqb, ik, iwb)
        allowed = key_pos[None, :] <= qpos[:, None]
        score = jnp.where(allowed[None], score, -jnp.inf)
        _, idx = lax.top_k(score, topk)
        valid = idx <= qpos[None, :, None]
        return sparse_attend(qb, _gather_rows(k, idx), _gather_rows(v, idx), valid)

    qpos_blocks = jnp.arange(seq).reshape(nb, Q_BLOCK)
    out = lax.map(one_block, (blocks(q), blocks(iq), blocks(iw), qpos_blocks))
    return jnp.moveaxis(out, 0, 1).reshape(bsz, seq, ATT_WIDTH)


def dsa_sample(q, k, v, iq, ik, iw, ck, cv, cik, page_table):
    bsz, t_len = q.shape[:2]
    past = page_table.shape[1] * PAGE_SIZE
    n_keys = past + t_len
    topk = min(TOPK_MAX, n_keys // 4)
    ik_past = cik[page_table].reshape(bsz, past, IDX_DIM).astype(ik.dtype)
    score = indexer_scores(iq, jnp.concatenate([ik_past, ik], axis=1), iw)
    qpos = past + jnp.arange(t_len)
    allowed = jnp.arange(n_keys)[None, :] <= qpos[:, None]
    score = jnp.where(allowed[None], score, -jnp.inf)
    _, idx = lax.top_k(score, topk)
    valid = idx <= qpos[None, :, None]
    from_new = idx >= past
    p_idx = jnp.minimum(idx, past - 1)
    phys = jax.vmap(lambda pt, i: pt[i // PAGE_SIZE])(page_table, p_idx)
    off = p_idx % PAGE_SIZE
    n_idx = jnp.clip(idx - past, 0, t_len - 1)

    def select(cache, new):
        return jnp.where(from_new[..., None, None], _gather_rows(new, n_idx), cache[phys, off].astype(new.dtype))

    return sparse_attend(q, select(ck, k), select(cv, v), valid)


def peer_ffn(xn, wq, keys, u, v):
    shp = xn.shape
    xt = xn.reshape(-1, D_MODEL)
    n = xt.shape[0]
    blk = min(PEER_BLOCK, n)
    nb = -(-n // blk)
    xt = jnp.pad(xt, ((0, nb * blk - n), (0, 0))).reshape(nb, blk, D_MODEL)

    def one(xb):
        qh = (xb @ wq).reshape(blk, PEER_HEADS, 2, PEER_HALF)
        s = jnp.einsum('thcd,hcnd->thcn', qh, keys).astype(jnp.float32)
        sv, si = lax.top_k(s, PEER_TOPK)
        cand = (sv[:, :, 0, :, None] + sv[:, :, 1, None, :]).reshape(blk, PEER_HEADS, -1)
        cidx = (si[:, :, 0, :, None] * N_KEYS + si[:, :, 1, None, :]).reshape(blk, PEER_HEADS, -1)
        fv, fi = lax.top_k(cand, PEER_TOPK)
        eidx = jnp.take_along_axis(cidx, fi, axis=-1)
        g = jax.nn.softmax(fv, axis=-1)
        act = jax.nn.gelu(jnp.einsum('thkd,td->thk', u[eidx], xb).astype(jnp.float32), approximate=False)
        return jnp.einsum('thk,thkd->td', (g * act).astype(xb.dtype), v[eidx])

    out = lax.map(one, xt).reshape(nb * blk, D_MODEL)[:n]
    return out.reshape(shp)


def layer_forward(x, pos, lw, conv_prefix, h0, paged):
    (norm1_g, w_in, conv_w, conv_b, dt_bias, a_log, d_skip, ssm_norm_g, q_norm_g, k_norm_g,
     w_branch_a, w_branch_b, w_out, norm2_g, peer_wq, peer_keys, peer_u, peer_v) = lw
    bsz, t_len, _ = x.shape
    xn = rms_norm(x, norm1_g)
    z, xbc, dt_raw, q, k, v, iq, ik, iw, gate_a, gate_b = jnp.split(xn @ w_in, _in_offsets(), axis=-1)
    y_a, h_new, conv_new = mamba2_branch(z, xbc, dt_raw, conv_prefix, h0, conv_w, conv_b, dt_bias,
                                         a_log, d_skip, ssm_norm_g, paged is None)
    q = rope(rms_norm(q.reshape(bsz, t_len, ATT_HEADS, HEAD_DIM), q_norm_g), pos)
    k = rope(rms_norm(k.reshape(bsz, t_len, KV_HEADS, HEAD_DIM), k_norm_g), pos)
    v = v.reshape(bsz, t_len, KV_HEADS, HEAD_DIM)
    iq = rope(iq.reshape(bsz, t_len, IDX_HEADS, IDX_DIM), pos)
    ik = rope(ik, pos)
    if paged is None:
        y_b = dsa_prompt(q, k, v, iq, ik, iw)
    else:
        y_b = dsa_sample(q, k, v, iq, ik, iw, *paged)
    mixed = jax.nn.sigmoid(gate_a) * (y_a @ w_branch_a) + jax.nn.sigmoid(gate_b) * (y_b @ w_branch_b)
    x = x + (mixed @ w_out).astype(x.dtype)
    x = x + peer_ffn(rms_norm(x, norm2_g), peer_wq, peer_keys, peer_u, peer_v).astype(x.dtype)
    h_out = h_new.reshape(bsz, SSM_HEADS, SSM_HEAD_DIM, D_STATE).astype(x.dtype)
    return x, k, v, ik, h_out, conv_new


def setup_inputs(seed: int = 0) -> dict:
    key = jax.random.key(seed)
    kk = jax.random.split(key, 32)
    f32 = jnp.float32

    def nrm(i, shape, scale):
        return scale * jax.random.normal(kk[i], shape, f32)

    n_pages = PAST_LEN // PAGE_SIZE
    n_used = DEC_BATCH * n_pages
    n_pool = n_used + max(1, n_used // 4)
    page_table = jax.random.permutation(kk[0], n_pool)[:n_used].reshape(DEC_BATCH, n_pages).astype(jnp.int32)
    dt0 = jnp.exp(jax.random.uniform(kk[1], (DEPTH, SSM_HEADS), f32, math.log(1e-3), math.log(1e-1)))
    dt_bias = dt0 + jnp.log(-jnp.expm1(-dt0))
    a_log = jnp.log(jax.random.uniform(kk[2], (DEPTH, SSM_HEADS), f32, 1.0, 16.0))
    return {
        'x_prompt': nrm(3, (BATCH, SEQ, D_MODEL), 1.0),
        'x_sample': nrm(4, (DEC_BATCH, DEC_SEQ, D_MODEL), 1.0),
        'cache_k': nrm(5, (DEPTH, n_pool, PAGE_SIZE, KV_HEADS, HEAD_DIM), 1.0),
        'cache_v': nrm(6, (DEPTH, n_pool, PAGE_SIZE, KV_HEADS, HEAD_DIM), 1.0),
        'cache_idx_k': nrm(7, (DEPTH, n_pool, PAGE_SIZE, IDX_DIM), 1.0),
        'state_ssm': nrm(8, (DEPTH, DEC_BATCH, SSM_HEADS, SSM_HEAD_DIM, D_STATE), 0.1),
        'state_conv': nrm(9, (DEPTH, DEC_BATCH, CONV_W - 1, CONV_DIM), 1.0),
        'page_table': page_table,
        'norm1_g': 1.0 + nrm(10, (DEPTH, D_MODEL), 0.02),
        'w_in': nrm(11, (DEPTH, D_MODEL, IN_WIDTH), D_MODEL ** -0.5),
        'conv_w': nrm(12, (DEPTH, CONV_W, CONV_DIM), CONV_W ** -0.5),
        'conv_b': nrm(13, (DEPTH, CONV_DIM), 0.02),
        'dt_bias': dt_bias,
        'a_log': a_log,
        'd_skip': 1.0 + nrm(14, (DEPTH, SSM_HEADS), 0.1),
        'ssm_norm_g': 1.0 + nrm(15, (DEPTH, D_INNER), 0.02),
        'q_norm_g': 1.0 + nrm(16, (DEPTH, HEAD_DIM), 0.02),
        'k_norm_g': 1.0 + nrm(17, (DEPTH, HEAD_DIM), 0.02),
        'w_branch_a': nrm(18, (DEPTH, D_INNER, D_MODEL), D_INNER ** -0.5),
        'w_branch_b': nrm(19, (DEPTH, ATT_WIDTH, D_MODEL), ATT_WIDTH ** -0.5),
        'w_out': nrm(20, (DEPTH, D_MODEL, D_MODEL), D_MODEL ** -0.5),
        'norm2_g': 1.0 + nrm(21, (DEPTH, D_MODEL), 0.02),
        'peer_wq': nrm(22, (DEPTH, D_MODEL, PEER_HEADS * PEER_KEY_DIM), D_MODEL ** -0.5),
        'peer_keys': nrm(23, (DEPTH, PEER_HEADS, 2, N_KEYS, PEER_HALF), PEER_HALF ** -0.5),
        'peer_u': nrm(24, (DEPTH, N_EXPERTS, D_MODEL), D_MODEL ** -0.5),
        'peer_v': nrm(25, (DEPTH, N_EXPERTS, D_MODEL), 0.1),
    }


def reference(x_prompt, x_sample, cache_k, cache_v, cache_idx_k, state_ssm, state_conv, page_table,
              norm1_g, w_in, conv_w, conv_b, dt_bias, a_log, d_skip, ssm_norm_g, q_norm_g, k_norm_g,
              w_branch_a, w_branch_b, w_out, norm2_g, peer_wq, peer_keys, peer_u, peer_v):
    bp, seq = x_prompt.shape[:2]
    bs, t_new = x_sample.shape[:2]
    past = page_table.shape[1] * PAGE_SIZE
    pos_p = jnp.arange(seq)
    pos_s = past + jnp.arange(t_new)
    yp, ys = x_prompt, x_sample
    kp, vp, ikp, hp, cp = [], [], [], [], []
    kq, vq, ikq, hq, cq = [], [], [], [], []
    for l in range(DEPTH):
        lw = (norm1_g[l], w_in[l], conv_w[l], conv_b[l], dt_bias[l], a_log[l], d_skip[l], ssm_norm_g[l],
              q_norm_g[l], k_norm_g[l], w_branch_a[l], w_branch_b[l], w_out[l], norm2_g[l],
              peer_wq[l], peer_keys[l], peer_u[l], peer_v[l])
        conv0 = jnp.zeros((bp, CONV_W - 1, CONV_DIM), x_prompt.dtype)
        h0 = jnp.zeros((bp, SSM_GROUPS, SSM_HPG, SSM_HEAD_DIM, D_STATE), jnp.float32)
        yp, k_, v_, ik_, h_, c_ = layer_forward(yp, pos_p, lw, conv0, h0, None)
        kp.append(k_); vp.append(v_); ikp.append(ik_); hp.append(h_); cp.append(c_)
        h_s = state_ssm[l].reshape(bs, SSM_GROUPS, SSM_HPG, SSM_HEAD_DIM, D_STATE).astype(jnp.float32)
        ys, k_, v_, ik_, h_, c_ = layer_forward(ys, pos_s, lw, state_conv[l], h_s,
                                                (cache_k[l], cache_v[l], cache_idx_k[l], page_table))
        kq.append(k_); vq.append(v_); ikq.append(ik_); hq.append(h_); cq.append(c_)
    return (yp, ys,
            jnp.stack(kp), jnp.stack(vp), jnp.stack(ikp), jnp.stack(hp), jnp.stack(cp),
            jnp.stack(kq), jnp.stack(vq), jnp.stack(ikq), jnp.stack(hq), jnp.stack(cq))
```

```python
import functools
import math

import jax
import jax.numpy as jnp
import numpy as np
from jax import lax
from jax.experimental import pallas as pl
from jax.experimental.pallas import tpu as pltpu

D_MODEL = 1024
DEPTH = 2
PAGE_SIZE = 128
D_INNER = 2048
SSM_HEAD_DIM = 64
SSM_HEADS = 32
SSM_GROUPS = 4
SSM_HPG = 8
D_STATE = 128
CONV_W = 4
CONV_DIM = D_INNER + 2 * SSM_GROUPS * D_STATE
SSD_CHUNK = 128
ATT_HEADS = 16
KV_HEADS = 4
HEAD_DIM = 64
ATT_REP = 4
ATT_WIDTH = 1024
KV_WIDTH = 256
IDX_HEADS = 8
IDX_DIM = 64
IDX_SCALE = (IDX_HEADS * IDX_DIM) ** -0.5
TOPK_MAX = 256
Q_BLOCK = 128
ROPE_THETA = 10000.0
PEER_HEADS = 8
N_KEYS = 128
PEER_KEY_DIM = 256
PEER_HALF = 128
PEER_TOPK = 16
PEER_BLOCK = 128
EPS = 1e-6
IN_SIZES = (D_INNER, CONV_DIM, SSM_HEADS, ATT_WIDTH, KV_WIDTH, KV_WIDTH,
            IDX_HEADS * IDX_DIM, IDX_DIM, IDX_HEADS, D_MODEL, D_MODEL)

F32 = jnp.float32
BF16 = jnp.bfloat16
VMEM_LIMIT = 56 * 1024 * 1024


def _mm_norm_kernel(x_ref, g_ref, w_ref, o_ref, xn_ref):
    @pl.when(pl.program_id(1) == 0)
    def _():
        x = x_ref[...]
        ms = jnp.mean(x * x, axis=-1, keepdims=True)
        xn_ref[...] = (x * lax.rsqrt(ms + EPS) * g_ref[...]).astype(BF16)

    o_ref[...] = jnp.dot(xn_ref[...], w_ref[...], preferred_element_type=F32)


def _mm_kernel(x_ref, w_ref, o_ref):
    o_ref[...] = jnp.dot(x_ref[...].astype(BF16), w_ref[...], preferred_element_type=F32)


def _pick_tile(n, candidates):
    for c in candidates:
        if n % c == 0:
            return c
    return n


def matmul(x, w, gain=None):
    m, k = x.shape
    n = w.shape[1]
    tm = _pick_tile(m, (1024, 512, 256, 128))
    tn = _pick_tile(n, (512, 256, 128))
    grid = (m // tm, n // tn)
    params = pltpu.CompilerParams(dimension_semantics=("parallel", "arbitrary"),
                                  vmem_limit_bytes=VMEM_LIMIT)
    if gain is None:
        return pl.pallas_call(
            _mm_kernel,
            out_shape=jax.ShapeDtypeStruct((m, n), F32),
            grid=grid,
            in_specs=[pl.BlockSpec((tm, k), lambda i, j: (i, 0)),
                      pl.BlockSpec((k, tn), lambda i, j: (0, j))],
            out_specs=pl.BlockSpec((tm, tn), lambda i, j: (i, j)),
            compiler_params=params,
        )(x, w)
    return pl.pallas_call(
        _mm_norm_kernel,
        out_shape=jax.ShapeDtypeStruct((m, n), F32),
        grid=grid,
        in_specs=[pl.BlockSpec((tm, k), lambda i, j: (i, 0)),
                  pl.BlockSpec((1, k), lambda i, j: (0, 0)),
                  pl.BlockSpec((k, tn), lambda i, j: (0, j))],
        out_specs=pl.BlockSpec((tm, tn), lambda i, j: (i, j)),
        scratch_shapes=[pltpu.VMEM((tm, k), BF16)],
        compiler_params=params,
    )(x, gain.reshape(1, k), w)


def _rms(x, g):
    xf = x.astype(F32)
    y = xf * lax.rsqrt(jnp.mean(xf * xf, axis=-1, keepdims=True) + EPS)
    return y * g.astype(F32)


def _rope(x, pos):
    d = x.shape[-1]
    inv = ROPE_THETA ** (-jnp.arange(0, d, 2, dtype=F32) / d)
    ang = pos.astype(F32)[:, None] * inv[None, :]
    ang = ang.reshape((ang.shape[0],) + (1,) * (x.ndim - 3) + (d // 2,))
    cos, sin = jnp.cos(ang), jnp.sin(ang)
    x1, x2 = x[..., :d // 2], x[..., d // 2:]
    return jnp.concatenate([x1 * cos - x2 * sin, x2 * cos + x1 * sin], axis=-1)


def _gather_rows(rows, idx):
    return jax.vmap(lambda r, i: r[i])(rows, idx)


def _ssd_chunked(xs, dt, a, bm, cm, h0):
    bsz, seq = xs.shape[:2]
    nc = seq // SSD_CHUNK

    def to_chunks(t):
        return jnp.moveaxis(t.reshape((bsz, nc, SSD_CHUNK) + t.shape[2:]), 1, 0)

    xd = xs * dt[..., None]
    ad = dt * a
    causal = jnp.tril(jnp.ones((SSD_CHUNK, SSD_CHUNK), dtype=bool))

    def step(h, inp):
        xd_c, ad_c, b_c, c_c = inp
        acs = jnp.cumsum(ad_c, axis=1)
        seg = acs[:, :, None] - acs[:, None, :]
        lmat = jnp.exp(jnp.where(causal[None, :, :, None, None], seg, -jnp.inf))
        cb = jnp.einsum('blgn,bsgn->blsg', c_c, b_c)
        y_diag = jnp.einsum('blsg,blsgr,bsgrp->blgrp', cb, lmat, xd_c)
        y_off = jnp.einsum('blgn,bgrpn,blgr->blgrp', c_c, h, jnp.exp(acs))
        decay_end = jnp.exp(acs[:, -1:] - acs)
        h_new = h * jnp.exp(acs[:, -1])[..., None, None] + jnp.einsum('bsgn,bsgr,bsgrp->bgrpn', b_c, decay_end, xd_c)
        return h_new, y_diag + y_off

    h, ys = lax.scan(step, h0, (to_chunks(xd), to_chunks(ad), to_chunks(bm), to_chunks(cm)))
    y = jnp.moveaxis(ys, 0, 1).reshape(xs.shape)
    return y, h


def _ssd_recurrent(xs, dt, a, bm, cm, h0):
    def step(h, inp):
        x_t, dt_t, b_t, c_t = inp
        h = h * jnp.exp(dt_t * a)[..., None, None] + jnp.einsum('bgrp,bgn->bgrpn', x_t * dt_t[..., None], b_t)
        return h, jnp.einsum('bgn,bgrpn->bgrp', c_t, h)

    tf = lambda t: jnp.moveaxis(t, 1, 0)
    h, ys = lax.scan(step, h0, (tf(xs), tf(dt), tf(bm), tf(cm)))
    return jnp.moveaxis(ys, 0, 1), h


def _mamba2(z, xbc, dt_raw, conv_prefix, h0, conv_w, conv_b, dt_bias, a_log, d_skip, norm_g, chunked):
    bsz, t_len, _ = z.shape
    xp = jnp.concatenate([conv_prefix, xbc], axis=1)
    conv = conv_b + sum(xp[:, j:j + t_len] * conv_w[j] for j in range(CONV_W))
    conv_new = xp[:, t_len:]
    xbc = jax.nn.silu(conv)
    xs, bm, cm = jnp.split(xbc, [D_INNER, D_INNER + SSM_GROUPS * D_STATE], axis=-1)
    xs = xs.reshape(bsz, t_len, SSM_GROUPS, SSM_HPG, SSM_HEAD_DIM)
    bm = bm.reshape(bsz, t_len, SSM_GROUPS, D_STATE)
    cm = cm.reshape(bsz, t_len, SSM_GROUPS, D_STATE)
    dt = jax.nn.softplus(dt_raw + dt_bias).reshape(bsz, t_len, SSM_GROUPS, SSM_HPG)
    a = -jnp.exp(a_log).reshape(SSM_GROUPS, SSM_HPG)
    scan_fn = _ssd_chunked if chunked else _ssd_recurrent
    y, h = scan_fn(xs, dt, a, bm, cm, h0)
    y = y + d_skip.reshape(SSM_GROUPS, SSM_HPG, 1) * xs
    y = y.reshape(bsz, t_len, D_INNER) * jax.nn.silu(z)
    y = _rms(y.reshape(bsz, t_len, SSM_GROUPS, D_INNER // SSM_GROUPS), norm_g.reshape(SSM_GROUPS, -1))
    return y.reshape(bsz, t_len, D_INNER), h, conv_new


def _indexer_scores(iq, ik, iw):
    dots = jax.nn.relu(jnp.einsum('bthd,bsd->bths', iq, ik))
    return jnp.einsum('bths,bth->bts', dots, iw * IDX_SCALE)


def _sparse_attend(q, ks, vs, valid):
    bsz, t_len = q.shape[:2]
    qg = q.reshape(bsz, t_len, KV_HEADS, ATT_REP, HEAD_DIM)
    s = jnp.einsum('btgrd,btkgd->btgrk', qg, ks) * (HEAD_DIM ** -0.5)
    s = jnp.where(valid[:, :, None, None, :], s, -jnp.inf)
    p = jax.nn.softmax(s, axis=-1)
    o = jnp.einsum('btgrk,btkgd->btgrd', p, vs)
    return o.reshape(bsz, t_len, ATT_WIDTH)


def _dsa_prompt(q, k, v, iq, ik, iw):
    bsz, seq = q.shape[:2]
    topk = min(TOPK_MAX, seq // 4)
    nb = seq // Q_BLOCK
    key_pos = jnp.arange(seq)

    def blocks(t):
        return jnp.moveaxis(t.reshape((bsz, nb, Q_BLOCK) + t.shape[2:]), 1, 0)

    def one_block(inp):
        qb, iqb, iwb, qpos = inp
        score = _indexer_scores(iqb, ik, iwb)
        allowed = key_pos[None, :] <= qpos[:, None]
        score = jnp.where(allowed[None], score, -jnp.inf)
        _, idx = lax.top_k(score, topk)
        valid = idx <= qpos[None, :, None]
        return _sparse_attend(qb, _gather_rows(k, idx), _gather_rows(v, idx), valid)

    qpos_blocks = jnp.arange(seq).reshape(nb, Q_BLOCK)
    out = lax.map(one_block, (blocks(q), blocks(iq), blocks(iw), qpos_blocks))
    return jnp.moveaxis(out, 0, 1).reshape(bsz, seq, ATT_WIDTH)


def _dsa_sample(q, k, v, iq, ik, iw, ck, cv, cik, page_table):
    bsz, t_len = q.shape[:2]
    past = page_table.shape[1] * PAGE_SIZE
    n_keys = past + t_len
    topk = min(TOPK_MAX, n_keys // 4)
    ik_past = cik[page_table].reshape(bsz, past, IDX_DIM)
    score = _indexer_scores(iq, jnp.concatenate([ik_past, ik], axis=1), iw)
    qpos = past + jnp.arange(t_len)
    allowed = jnp.arange(n_keys)[None, :] <= qpos[:, None]
    score = jnp.where(allowed[None], score, -jnp.inf)
    _, idx = lax.top_k(score, topk)
    valid = idx <= qpos[None, :, None]
    from_new = idx >= past
    p_idx = jnp.minimum(idx, past - 1)
    phys = jax.vmap(lambda pt, i: pt[i // PAGE_SIZE])(page_table, p_idx)
    off = p_idx % PAGE_SIZE
    n_idx = jnp.clip(idx - past, 0, t_len - 1)

    def select(cache, new):
        return jnp.where(from_new[..., None, None], _gather_rows(new, n_idx), cache[phys, off])

    return _sparse_attend(q, select(ck, k), select(cv, v), valid)


def _peer(xn2, wq_bf, keys, u, v):
    n = xn2.shape[0]
    blk = min(PEER_BLOCK, n)
    nb = n // blk
    qall = matmul(xn2, wq_bf)
    xt = xn2.reshape(nb, blk, D_MODEL)
    qt = qall.reshape(nb, blk, PEER_HEADS * PEER_KEY_DIM)

    def one(inp):
        xb, qb = inp
        qh = qb.reshape(blk, PEER_HEADS, 2, PEER_HALF)
        s = jnp.einsum('thcd,hcnd->thcn', qh, keys)
        sv, si = lax.top_k(s, PEER_TOPK)
        cand = (sv[:, :, 0, :, None] + sv[:, :, 1, None, :]).reshape(blk, PEER_HEADS, -1)
        cidx = (si[:, :, 0, :, None] * N_KEYS + si[:, :, 1, None, :]).reshape(blk, PEER_HEADS, -1)
        fv, fi = lax.top_k(cand, PEER_TOPK)
        eidx = jnp.take_along_axis(cidx, fi, axis=-1)
        g = jax.nn.softmax(fv, axis=-1)
        act = jax.nn.gelu(jnp.einsum('thkd,td->thk', u[eidx], xb), approximate=False)
        return jnp.einsum('thk,thkd->td', g * act, v[eidx])

    return lax.map(one, (xt, qt)).reshape(n, D_MODEL)


def _layer(x, pos, lw, conv_prefix, h0, paged):
    (norm1_g, w_in_bf, conv_w, conv_b, dt_bias, a_log, d_skip, ssm_norm_g, q_norm_g, k_norm_g,
     w_a_bf, w_b_bf, w_out_bf, norm2_g, wq_bf, peer_keys, peer_u, peer_v) = lw
    bsz, t_len, _ = x.shape
    n = bsz * t_len
    x2 = x.reshape(n, D_MODEL)
    proj = matmul(x2, w_in_bf, gain=norm1_g)[:, :sum(IN_SIZES)].reshape(bsz, t_len, -1)
    offs = [int(o) for o in np.cumsum(IN_SIZES)[:-1]]
    z, xbc, dt_raw, q, k, v, iq, ik, iw, gate_a, gate_b = jnp.split(proj, offs, axis=-1)
    y_a, h_new, conv_new = _mamba2(z, xbc, dt_raw, conv_prefix, h0, conv_w, conv_b, dt_bias,
                                   a_log, d_skip, ssm_norm_g, paged is None)
    q = _rope(_rms(q.reshape(bsz, t_len, ATT_HEADS, HEAD_DIM), q_norm_g), pos)
    k = _rope(_rms(k.reshape(bsz, t_len, KV_HEADS, HEAD_DIM), k_norm_g), pos)
    v = v.reshape(bsz, t_len, KV_HEADS, HEAD_DIM)
    iq = _rope(iq.reshape(bsz, t_len, IDX_HEADS, IDX_DIM), pos)
    ik = _rope(ik, pos)
    if paged is None:
        y_b = _dsa_prompt(q, k, v, iq, ik, iw)
    else:
        y_b = _dsa_sample(q, k, v, iq, ik, iw, *paged)
    ma = matmul(y_a.reshape(n, D_INNER), w_a_bf)
    mb = matmul(y_b.reshape(n, ATT_WIDTH), w_b_bf)
    mixed = jax.nn.sigmoid(gate_a.reshape(n, D_MODEL)) * ma + jax.nn.sigmoid(gate_b.reshape(n, D_MODEL)) * mb
    x2 = x2 + matmul(mixed, w_out_bf)
    xn2 = _rms(x2, norm2_g)
    x2 = x2 + _peer(xn2, wq_bf, peer_keys, peer_u, peer_v)
    h_out = h_new.reshape(bsz, SSM_HEADS, SSM_HEAD_DIM, D_STATE)
    return x2.reshape(bsz, t_len, D_MODEL), k, v, ik, h_out, conv_new


def kernel(x_prompt, x_sample, cache_k, cache_v, cache_idx_k, state_ssm, state_conv, page_table,
           norm1_g, w_in, conv_w, conv_b, dt_bias, a_log, d_skip, ssm_norm_g, q_norm_g, k_norm_g,
           w_branch_a, w_branch_b, w_out, norm2_g, peer_wq, peer_keys, peer_u, peer_v):
    bp, seq = x_prompt.shape[:2]
    bs, t_new = x_sample.shape[:2]
    past = page_table.shape[1] * PAGE_SIZE
    pos_p = jnp.arange(seq)
    pos_s = past + jnp.arange(t_new)
    yp, ys = x_prompt, x_sample
    outs_p, outs_s = [], []
    for l in range(DEPTH):
        w_in_pad = jnp.pad(w_in[l].astype(BF16), ((0, 0), (0, 9728 - sum(IN_SIZES))))
        lw = (norm1_g[l], w_in_pad, conv_w[l], conv_b[l], dt_bias[l], a_log[l], d_skip[l],
              ssm_norm_g[l], q_norm_g[l], k_norm_g[l], w_branch_a[l].astype(BF16),
              w_branch_b[l].astype(BF16), w_out[l].astype(BF16), norm2_g[l],
              peer_wq[l].astype(BF16), peer_keys[l], peer_u[l], peer_v[l])
        conv0 = jnp.zeros((bp, CONV_W - 1, CONV_DIM), F32)
        h0 = jnp.zeros((bp, SSM_GROUPS, SSM_HPG, SSM_HEAD_DIM, D_STATE), F32)
        yp, *rest = _layer(yp, pos_p, lw, conv0, h0, None)
        outs_p.append(rest)
        h_s = state_ssm[l].reshape(bs, SSM_GROUPS, SSM_HPG, SSM_HEAD_DIM, D_STATE)
        ys, *rest = _layer(ys, pos_s, lw, state_conv[l], h_s,
                           (cache_k[l], cache_v[l], cache_idx_k[l], page_table))
        outs_s.append(rest)
    stack = lambda outs, i: jnp.stack([o[i] for o in outs])
    return (yp, ys,
            stack(outs_p, 0), stack(outs_p, 1), stack(outs_p, 2), stack(outs_p, 3), stack(outs_p, 4),
            stack(outs_s, 0), stack(outs_s, 1), stack(outs_s, 2), stack(outs_s, 3), stack(outs_s, 4))
```

```python
import functools
import math

import jax
import jax.numpy as jnp
import numpy as np
from jax import lax
from jax.experimental import pallas as pl
from jax.experimental.pallas import tpu as pltpu

D_MODEL = 1024
DEPTH = 2
PAGE_SIZE = 128
D_INNER = 2048
SSM_HEAD_DIM = 64
SSM_HEADS = 32
SSM_GROUPS = 4
SSM_HPG = 8
D_STATE = 128
CONV_W = 4
CONV_DIM = D_INNER + 2 * SSM_GROUPS * D_STATE
SSD_CHUNK = 128
ATT_HEADS = 16
KV_HEADS = 4
HEAD_DIM = 64
ATT_REP = 4
ATT_WIDTH = 1024
KV_WIDTH = 256
IDX_HEADS = 8
IDX_DIM = 64
IDX_SCALE = (IDX_HEADS * IDX_DIM) ** -0.5
TOPK_MAX = 256
Q_BLOCK = 128
ROPE_THETA = 10000.0
PEER_HEADS = 8
N_KEYS = 128
N_EXPERTS = N_KEYS * N_KEYS
PEER_KEY_DIM = 256
PEER_HALF = 128
PEER_TOPK = 16
PEER_PAIRS = PEER_HEADS * PEER_TOPK
EPS = 1e-6
IN_SIZES = (D_INNER, CONV_DIM, SSM_HEADS, ATT_WIDTH, KV_WIDTH, KV_WIDTH,
            IDX_HEADS * IDX_DIM, IDX_DIM, IDX_HEADS, D_MODEL, D_MODEL)
IN_WIDTH = sum(IN_SIZES)
IN_WIDTH_PAD = 9728

F32 = jnp.float32
BF16 = jnp.bfloat16
I32 = jnp.int32
INT_MIN = np.int32(-2 ** 31)
VMEM_LIMIT = 56 * 1024 * 1024
LANES = 128
NT_DIMS = (((1,), (1,)), ((), ()))


def _pick_tile(n, candidates):
    for c in candidates:
        if n % c == 0:
            return c
    return n


def _mm_norm_kernel(x_ref, g_ref, w_ref, o_ref, xn_ref):
    @pl.when(pl.program_id(1) == 0)
    def _():
        x = x_ref[...]
        ms = jnp.mean(x * x, axis=-1, keepdims=True)
        xn_ref[...] = (x * lax.rsqrt(ms + EPS) * g_ref[...]).astype(BF16)

    o_ref[...] = jnp.dot(xn_ref[...], w_ref[...], preferred_element_type=F32)


def _mm_kernel(x_ref, w_ref, o_ref):
    o_ref[...] = jnp.dot(x_ref[...].astype(BF16), w_ref[...], preferred_element_type=F32)


def matmul(x, w):
    m, k = x.shape
    n = w.shape[1]
    tm = _pick_tile(m, (1024, 512, 256, 128))
    tn = _pick_tile(n, (512, 256, 128))
    return pl.pallas_call(
        _mm_kernel,
        out_shape=jax.ShapeDtypeStruct((m, n), F32),
        grid=(m // tm, n // tn),
        in_specs=[pl.BlockSpec((tm, k), lambda i, j: (i, 0)),
                  pl.BlockSpec((k, tn), lambda i, j: (0, j))],
        out_specs=pl.BlockSpec((tm, tn), lambda i, j: (i, j)),
        compiler_params=pltpu.CompilerParams(dimension_semantics=("parallel", "arbitrary"),
                                             vmem_limit_bytes=VMEM_LIMIT),
        name="matmul",
    )(x, w)


def norm_matmul(x, gain, w):
    m, k = x.shape
    n = w.shape[1]
    tm = _pick_tile(m, (1024, 512, 256, 128))
    tn = _pick_tile(n, (512, 256, 128))
    return pl.pallas_call(
        _mm_norm_kernel,
        out_shape=(jax.ShapeDtypeStruct((m, n), F32), jax.ShapeDtypeStruct((m, k), BF16)),
        grid=(m // tm, n // tn),
        in_specs=[pl.BlockSpec((tm, k), lambda i, j: (i, 0)),
                  pl.BlockSpec((1, k), lambda i, j: (0, 0)),
                  pl.BlockSpec((k, tn), lambda i, j: (0, j))],
        out_specs=(pl.BlockSpec((tm, tn), lambda i, j: (i, j)),
                   pl.BlockSpec((tm, k), lambda i, j: (i, 0))),
        compiler_params=pltpu.CompilerParams(dimension_semantics=("parallel", "arbitrary"),
                                             vmem_limit_bytes=VMEM_LIMIT),
        name="norm_matmul",
    )(x, gain.reshape(1, k), w)


def _top16_rows(s):
    r = s.shape[0]
    rows = lax.broadcasted_iota(I32, s.shape, 0)
    vals, idxs = [], []
    for _ in range(PEER_TOPK):
        m = jnp.max(s, axis=0, keepdims=True)
        i = jnp.min(jnp.where(s == m, rows, r), axis=0, keepdims=True)
        vals.append(m)
        idxs.append(i)
        s = jnp.where(rows == i, -jnp.inf, s)
    return jnp.concatenate(vals, axis=0), jnp.concatenate(idxs, axis=0)


def _peer_pairs(sv0, si0, sv1, si1):
    t = sv0.shape[1]
    row8 = lax.broadcasted_iota(I32, (8, t), 0)
    row16 = lax.broadcasted_iota(I32, (16, t), 0)
    vals = [sv0[0:1] + sv1]
    eids = [si0[0:1] * N_KEYS + si1]
    flat = [row16]
    for a in range(1, 8):
        nb = PEER_TOPK // (a + 1)
        v = sv0[a:a + 1] + sv1[0:8]
        vals.append(v if nb >= 8 else jnp.where(row8 < nb, v, -jnp.inf))
        eids.append(si0[a:a + 1] * N_KEYS + si1[0:8])
        flat.append(row8 + PEER_TOPK * a)
    vals.append(sv0[8:16] + sv1[0:1])
    eids.append(si0[8:16] * N_KEYS + si1[0:1])
    flat.append((row8 + 8) * PEER_TOPK)
    cand = jnp.concatenate(vals, axis=0)
    ce = jnp.concatenate(eids, axis=0)
    cf = jnp.concatenate(flat, axis=0)
    fv, fe = [], []
    for _ in range(PEER_TOPK):
        m = jnp.max(cand, axis=0, keepdims=True)
        pick = jnp.min(jnp.where(cand == m, cf, PEER_TOPK * PEER_TOPK), axis=0, keepdims=True)
        sel = cf == pick
        fe.append(jnp.max(jnp.where(sel, ce, -1), axis=0, keepdims=True))
        fv.append(m)
        cand = jnp.where(sel, -jnp.inf, cand)
    fv = jnp.concatenate(fv, axis=0)
    fe = jnp.concatenate(fe, axis=0)
    p = jnp.exp(fv - fv[0:1])
    return fe, p / jnp.sum(p, axis=0, keepdims=True)


def _peer_topk_kernel(q_ref, keys_ref, a_ref, b_ref, g_ref):
    q = q_ref[...].astype(BF16)
    es, gs = [], []
    for h in range(PEER_HEADS):
        tops = []
        for c in range(2):
            hc = 2 * h + c
            s = lax.dot_general(keys_ref[hc], q[:, hc * PEER_HALF:(hc + 1) * PEER_HALF], NT_DIMS,
                                preferred_element_type=F32)
            tops.append(_top16_rows(s))
        e, g = _peer_pairs(tops[0][0], tops[0][1], tops[1][0], tops[1][1])
        es.append(e)
        gs.append(g)
    e = jnp.concatenate(es, axis=0)
    g = jnp.concatenate(gs, axis=0)
    a_ref[...] = (e >> 7).astype(F32).T
    b_ref[...] = (e & (N_KEYS - 1)).astype(F32).T
    g_ref[...] = g.T


PEER_TOPK_TILE = 128


def peer_topk(q, keys_bf):
    n = q.shape[0]
    tt = PEER_TOPK_TILE
    spec = pl.BlockSpec((tt, PEER_PAIRS), lambda i: (i, 0))
    shp = jax.ShapeDtypeStruct((n, PEER_PAIRS), F32)
    return pl.pallas_call(
        _peer_topk_kernel,
        out_shape=(shp, shp, shp),
        grid=(n // tt,),
        in_specs=[pl.BlockSpec((tt, PEER_HEADS * PEER_KEY_DIM), lambda i: (i, 0)),
                  pl.BlockSpec((2 * PEER_HEADS, N_KEYS, PEER_HALF), lambda i: (0, 0, 0))],
        out_specs=(spec, spec, spec),
        compiler_params=pltpu.CompilerParams(dimension_semantics=("parallel",),
                                             vmem_limit_bytes=VMEM_LIMIT),
        name="peer_topk",
    )(q, keys_bf)


W_PITCH = 136
PEER_W_TILE = 64


def _peer_w_kernel(a_ref, b_ref, g_ref, o_ref, wf_ref):
    tt = a_ref.shape[0]
    ii = lax.broadcasted_iota(I32, (N_KEYS, PEER_PAIRS), 0).astype(F32)

    def body(t, carry):
        a = a_ref[pl.ds(t, 1), :]
        b = b_ref[pl.ds(t, 1), :]
        g = g_ref[pl.ds(t, 1), :]
        at = jnp.where(ii == a, 1.0, 0.0).astype(BF16)
        bt = jnp.where(ii == b, g, 0.0).astype(BF16)
        w = lax.dot_general(at, bt, NT_DIMS, preferred_element_type=F32)
        wf_ref[pl.ds(pl.multiple_of(t * W_PITCH, 8), N_KEYS), :] = w
        return carry

    lax.fori_loop(0, tt, body, 0)
    for i1 in range(N_KEYS):
        o_ref[:, i1 * N_KEYS:(i1 + 1) * N_KEYS] = wf_ref[pl.ds(i1, tt, stride=W_PITCH), :].astype(BF16)


def peer_gate_matrix(a, b, g):
    n = a.shape[0]
    tt = PEER_W_TILE
    spec = pl.BlockSpec((tt, PEER_PAIRS), lambda i: (i, 0))
    return pl.pallas_call(
        _peer_w_kernel,
        out_shape=jax.ShapeDtypeStruct((n, N_EXPERTS), BF16),
        grid=(n // tt,),
        in_specs=[spec, spec, spec],
        out_specs=pl.BlockSpec((tt, N_EXPERTS), lambda i: (i, 0)),
        scratch_shapes=[pltpu.VMEM((tt * W_PITCH, N_KEYS), F32)],
        compiler_params=pltpu.CompilerParams(dimension_semantics=("parallel",),
                                             vmem_limit_bytes=VMEM_LIMIT),
        name="peer_gate_matrix",
    )(a, b, g)


def _peer_dense_kernel(xn_ref, x_ref, ut_ref, w_ref, v_ref, o_ref):
    @pl.when(pl.program_id(1) == 0)
    def _():
        o_ref[...] = x_ref[...]

    h = jnp.dot(xn_ref[...], ut_ref[...], preferred_element_type=F32)
    act = 0.5 * h * (1.0 + lax.erf(h * np.float32(math.sqrt(0.5))))
    wa = (w_ref[...].astype(F32) * act).astype(BF16)
    o_ref[...] += jnp.dot(wa, v_ref[...], preferred_element_type=F32)


PEER_EXPERT_TILE = 1024


def peer_dense(xn, x, ut, w, v):
    n = x.shape[0]
    tm = _pick_tile(n, (1024, 512, 256, 128))
    te = PEER_EXPERT_TILE
    return pl.pallas_call(
        _peer_dense_kernel,
        out_shape=jax.ShapeDtypeStruct((n, D_MODEL), F32),
        grid=(n // tm, N_EXPERTS // te),
        in_specs=[pl.BlockSpec((tm, D_MODEL), lambda i, j: (i, 0)),
                  pl.BlockSpec((tm, D_MODEL), lambda i, j: (i, 0)),
                  pl.BlockSpec((D_MODEL, te), lambda i, j: (0, j)),
                  pl.BlockSpec((tm, te), lambda i, j: (i, j)),
                  pl.BlockSpec((te, D_MODEL), lambda i, j: (j, 0))],
        out_specs=pl.BlockSpec((tm, D_MODEL), lambda i, j: (i, 0)),
        compiler_params=pltpu.CompilerParams(dimension_semantics=("parallel", "arbitrary"),
                                             vmem_limit_bytes=VMEM_LIMIT),
        name="peer_dense",
    )(xn, x, ut, w, v)


def peer_block(x2, norm2_g, wq_bf, keys_bf, ut_bf, v_bf):
    n = x2.shape[0]
    npad = -(-n // PEER_TOPK_TILE) * PEER_TOPK_TILE
    xp = jnp.pad(x2, ((0, npad - n), (0, 0))) if npad != n else x2
    q, xn = norm_matmul(xp, norm2_g, wq_bf)
    a, b, g = peer_topk(q, keys_bf)
    w = peer_gate_matrix(a, b, g)
    out = peer_dense(xn, xp, ut_bf, w, v_bf)
    return out[:n] if npad != n else out


def _dsa_prompt_kernel(iq_ref, iw_ref, ik_ref, q_ref, k_ref, v_ref, o_ref, jc_ref):
    qb = pl.program_id(1)
    s_len = ik_ref.shape[0]
    ik = ik_ref[...].astype(BF16)
    iw = iw_ref[...] * np.float32(IDX_SCALE)
    score = jnp.zeros((Q_BLOCK, s_len), F32)
    for h in range(IDX_HEADS):
        iqh = iq_ref[:, h * IDX_DIM:(h + 1) * IDX_DIM].astype(BF16)
        d = lax.dot_general(iqh, ik, NT_DIMS, preferred_element_type=F32)
        score = score + jnp.maximum(d, 0.0) * iw[:, h:h + 1]

    tpos = qb * Q_BLOCK + lax.broadcasted_iota(I32, (Q_BLOCK, 1), 0)
    spos = lax.broadcasted_iota(I32, (1, s_len), 1)
    causal = spos <= tpos
    bits = lax.bitcast_convert_type(score, I32)
    key = bits ^ ((bits >> 31) & np.int32(0x7FFFFFFF))
    key = jnp.where(score == 0.0, 0, key)
    key = jnp.where(causal, key, INT_MIN)

    def count(mask):
        return jnp.sum(mask.astype(I32), axis=-1, keepdims=True)

    def bisect_value(i, tu):
        cand_u = tu | jnp.left_shift(np.int32(1), 31 - i)
        return jnp.where(count(key >= (cand_u ^ INT_MIN)) >= TOPK_MAX, cand_u, tu)

    thr = lax.fori_loop(0, 32, bisect_value, jnp.zeros((Q_BLOCK, 1), I32)) ^ INT_MIN
    above = key > thr
    tied = key == thr
    need = TOPK_MAX - count(above)
    excess = (count(tied) > need) & (thr != INT_MIN)

    jc_ref[...] = jnp.full((Q_BLOCK, LANES), s_len, I32)

    @pl.when(jnp.max(excess.astype(I32)) > 0)
    def _():
        def bisect_pos(i, j):
            cj = j | jnp.left_shift(np.int32(1), 10 - i)
            return jnp.where(count(tied & (spos < cj)) < need, cj, j)

        jc = lax.fori_loop(0, 11, bisect_pos, jnp.zeros((Q_BLOCK, 1), I32))
        jc_ref[...] = jnp.broadcast_to(jc, (Q_BLOCK, LANES))

    jc = jc_ref[:, 0:1]
    sel = causal & (above | (tied & (spos <= jc)))
    bias = jnp.where(sel, 0.0, -jnp.inf)

    for g in range(KV_HEADS):
        qs = jnp.concatenate(
            [q_ref[:, (ATT_REP * g + r) * HEAD_DIM:(ATT_REP * g + r + 1) * HEAD_DIM] for r in range(ATT_REP)],
            axis=0).astype(BF16)
        kg = k_ref[:, g * HEAD_DIM:(g + 1) * HEAD_DIM].astype(BF16)
        vg = v_ref[:, g * HEAD_DIM:(g + 1) * HEAD_DIM].astype(BF16)
        s = lax.dot_general(qs, kg, NT_DIMS, preferred_element_type=F32) * np.float32(HEAD_DIM ** -0.5)
        s = (s.reshape(ATT_REP, Q_BLOCK, s_len) + bias[None]).reshape(ATT_REP * Q_BLOCK, s_len)
        m = jnp.max(s, axis=-1, keepdims=True)
        p = jnp.exp(s - m)
        l = jnp.sum(p, axis=-1, keepdims=True)
        o = jnp.dot(p.astype(BF16), vg, preferred_element_type=F32) / l
        for r in range(ATT_REP):
            hh = ATT_REP * g + r
            o_ref[:, hh * HEAD_DIM:(hh + 1) * HEAD_DIM] = o[r * Q_BLOCK:(r + 1) * Q_BLOCK]


def dsa_prompt(q, k, v, iq, ik, iw, bsz, seq):
    nq = seq // Q_BLOCK
    assert seq == 2048 and TOPK_MAX <= seq // 4
    qmap = lambda b, i: (b * nq + i, 0)
    bmap = lambda b, i: (b, 0)
    return pl.pallas_call(
        _dsa_prompt_kernel,
        out_shape=jax.ShapeDtypeStruct((bsz * seq, ATT_WIDTH), F32),
        grid=(bsz, nq),
        in_specs=[pl.BlockSpec((Q_BLOCK, IDX_HEADS * IDX_DIM), qmap),
                  pl.BlockSpec((Q_BLOCK, IDX_HEADS), qmap),
                  pl.BlockSpec((seq, IDX_DIM), bmap),
                  pl.BlockSpec((Q_BLOCK, ATT_WIDTH), qmap),
                  pl.BlockSpec((seq, KV_WIDTH), bmap),
                  pl.BlockSpec((seq, KV_WIDTH), bmap)],
        out_specs=pl.BlockSpec((Q_BLOCK, ATT_WIDTH), qmap),
        scratch_shapes=[pltpu.VMEM((Q_BLOCK, LANES), I32)],
        compiler_params=pltpu.CompilerParams(dimension_semantics=("parallel", "arbitrary"),
                                             vmem_limit_bytes=VMEM_LIMIT),
        name="dsa_prompt",
    )(iq, iw, ik, q, k, v)


def _rms(x, g):
    xf = x.astype(F32)
    y = xf * lax.rsqrt(jnp.mean(xf * xf, axis=-1, keepdims=True) + EPS)
    return y * g.astype(F32)


def _rope(x, pos):
    d = x.shape[-1]
    inv = ROPE_THETA ** (-jnp.arange(0, d, 2, dtype=F32) / d)
    ang = pos.astype(F32)[:, None] * inv[None, :]
    ang = ang.reshape((ang.shape[0],) + (1,) * (x.ndim - 3) + (d // 2,))
    cos, sin = jnp.cos(ang), jnp.sin(ang)
    x1, x2 = x[..., :d // 2], x[..., d // 2:]
    return jnp.concatenate([x1 * cos - x2 * sin, x2 * cos + x1 * sin], axis=-1)


def _gather_rows(rows, idx):
    return jax.vmap(lambda r, i: r[i])(rows, idx)


def _ssd_chunked(xs, dt, a, bm, cm, h0):
    bsz, seq = xs.shape[:2]
    nc = seq // SSD_CHUNK

    def to_chunks(t):
        return jnp.moveaxis(t.reshape((bsz, nc, SSD_CHUNK) + t.shape[2:]), 1, 0)

    xd = xs * dt[..., None]
    ad = dt * a
    causal = jnp.tril(jnp.ones((SSD_CHUNK, SSD_CHUNK), dtype=bool))

    def step(h, inp):
        xd_c, ad_c, b_c, c_c = inp
        acs = jnp.cumsum(ad_c, axis=1)
        seg = acs[:, :, None] - acs[:, None, :]
        lmat = jnp.exp(jnp.where(causal[None, :, :, None, None], seg, -jnp.inf))
        cb = jnp.einsum('blgn,bsgn->blsg', c_c, b_c)
        y_diag = jnp.einsum('blsg,blsgr,bsgrp->blgrp', cb, lmat, xd_c)
        y_off = jnp.einsum('blgn,bgrpn,blgr->blgrp', c_c, h, jnp.exp(acs))
        decay_end = jnp.exp(acs[:, -1:] - acs)
        h_new = h * jnp.exp(acs[:, -1])[..., None, None] + jnp.einsum('bsgn,bsgr,bsgrp->bgrpn', b_c, decay_end, xd_c)
        return h_new, y_diag + y_off

    h, ys = lax.scan(step, h0, (to_chunks(xd), to_chunks(ad), to_chunks(bm), to_chunks(cm)))
    y = jnp.moveaxis(ys, 0, 1).reshape(xs.shape)
    return y, h


def _ssd_recurrent(xs, dt, a, bm, cm, h0):
    def step(h, inp):
        x_t, dt_t, b_t, c_t = inp
        h = h * jnp.exp(dt_t * a)[..., None, None] + jnp.einsum('bgrp,bgn->bgrpn', x_t * dt_t[..., None], b_t)
        return h, jnp.einsum('bgn,bgrpn->bgrp', c_t, h)

    tf = lambda t: jnp.moveaxis(t, 1, 0)
    h, ys = lax.scan(step, h0, (tf(xs), tf(dt), tf(bm), tf(cm)))
    return jnp.moveaxis(ys, 0, 1), h


def _mamba2(z, xbc, dt_raw, conv_prefix, h0, conv_w, conv_b, dt_bias, a_log, d_skip, norm_g, chunked):
    bsz, t_len, _ = z.shape
    xp = jnp.concatenate([conv_prefix, xbc], axis=1)
    conv = conv_b + sum(xp[:, j:j + t_len] * conv_w[j] for j in range(CONV_W))
    conv_new = xp[:, t_len:]
    xbc = jax.nn.silu(conv)
    xs, bm, cm = jnp.split(xbc, [D_INNER, D_INNER + SSM_GROUPS * D_STATE], axis=-1)
    xs = xs.reshape(bsz, t_len, SSM_GROUPS, SSM_HPG, SSM_HEAD_DIM)
    bm = bm.reshape(bsz, t_len, SSM_GROUPS, D_STATE)
    cm = cm.reshape(bsz, t_len, SSM_GROUPS, D_STATE)
    dt = jax.nn.softplus(dt_raw + dt_bias).reshape(bsz, t_len, SSM_GROUPS, SSM_HPG)
    a = -jnp.exp(a_log).reshape(SSM_GROUPS, SSM_HPG)
    scan_fn = _ssd_chunked if chunked else _ssd_recurrent
    y, h = scan_fn(xs, dt, a, bm, cm, h0)
    y = y + d_skip.reshape(SSM_GROUPS, SSM_HPG, 1) * xs
    y = y.reshape(bsz, t_len, D_INNER) * jax.nn.silu(z)
    y = _rms(y.reshape(bsz, t_len, SSM_GROUPS, D_INNER // SSM_GROUPS), norm_g.reshape(SSM_GROUPS, -1))
    return y.reshape(bsz, t_len, D_INNER), h, conv_new


def _indexer_scores(iq, ik, iw):
    dots = jax.nn.relu(jnp.einsum('bthd,bsd->bths', iq, ik))
    return jnp.einsum('bths,bth->bts', dots, iw * IDX_SCALE)


def _sparse_attend(q, ks, vs, valid):
    bsz, t_len = q.shape[:2]
    qg = q.reshape(bsz, t_len, KV_HEADS, ATT_REP, HEAD_DIM)
    s = jnp.einsum('btgrd,btkgd->btgrk', qg, ks) * (HEAD_DIM ** -0.5)
    s = jnp.where(valid[:, :, None, None, :], s, -jnp.inf)
    p = jax.nn.softmax(s, axis=-1)
    o = jnp.einsum('btgrk,btkgd->btgrd', p, vs)
    return o.reshape(bsz, t_len, ATT_WIDTH)


def _dsa_sample(q, k, v, iq, ik, iw, ck, cv, cik, page_table):
    bsz, t_len = q.shape[:2]
    past = page_table.shape[1] * PAGE_SIZE
    n_keys = past + t_len
    topk = min(TOPK_MAX, n_keys // 4)
    ik_past = cik[page_table].reshape(bsz, past, IDX_DIM)
    score = _indexer_scores(iq, jnp.concatenate([ik_past, ik], axis=1), iw)
    qpos = past + jnp.arange(t_len)
    allowed = jnp.arange(n_keys)[None, :] <= qpos[:, None]
    score = jnp.where(allowed[None], score, -jnp.inf)
    _, idx = lax.top_k(score, topk)
    valid = idx <= qpos[None, :, None]
    from_new = idx >= past
    p_idx = jnp.minimum(idx, past - 1)
    phys = jax.vmap(lambda pt, i: pt[i // PAGE_SIZE])(page_table, p_idx)
    off = p_idx % PAGE_SIZE
    n_idx = jnp.clip(idx - past, 0, t_len - 1)

    def select(cache, new):
        return jnp.where(from_new[..., None, None], _gather_rows(new, n_idx), cache[phys, off])

    return _sparse_attend(q, select(ck, k), select(cv, v), valid)


def _layer(x, pos, lw, conv_prefix, h0, paged):
    (norm1_g, w_in_bf, conv_w, conv_b, dt_bias, a_log, d_skip, ssm_norm_g, q_norm_g, k_norm_g,
     w_a_bf, w_b_bf, w_out_bf, norm2_g, wq_bf, keys_bf, ut_bf, v_bf) = lw
    bsz, t_len, _ = x.shape
    n = bsz * t_len
    x2 = x.reshape(n, D_MODEL)
    proj, _ = norm_matmul(x2, norm1_g, w_in_bf)
    proj = proj[:, :IN_WIDTH].reshape(bsz, t_len, -1)
    offs = [int(o) for o in np.cumsum(IN_SIZES)[:-1]]
    z, xbc, dt_raw, q, k, v, iq, ik, iw, gate_a, gate_b = jnp.split(proj, offs, axis=-1)
    y_a, h_new, conv_new = _mamba2(z, xbc, dt_raw, conv_prefix, h0, conv_w, conv_b, dt_bias,
                                   a_log, d_skip, ssm_norm_g, paged is None)
    q = _rope(_rms(q.reshape(bsz, t_len, ATT_HEADS, HEAD_DIM), q_norm_g), pos)
    k = _rope(_rms(k.reshape(bsz, t_len, KV_HEADS, HEAD_DIM), k_norm_g), pos)
    v = v.reshape(bsz, t_len, KV_HEADS, HEAD_DIM)
    iq = _rope(iq.reshape(bsz, t_len, IDX_HEADS, IDX_DIM), pos)
    ik = _rope(ik, pos)
    if paged is None:
        y_b = dsa_prompt(q.reshape(n, ATT_WIDTH), k.reshape(n, KV_WIDTH), v.reshape(n, KV_WIDTH),
                         iq.reshape(n, IDX_HEADS * IDX_DIM), ik.reshape(n, IDX_DIM),
                         iw.reshape(n, IDX_HEADS), bsz, t_len)
    else:
        y_b = _dsa_sample(q, k, v, iq, ik, iw, *paged).reshape(n, ATT_WIDTH)
    ma = matmul(y_a.reshape(n, D_INNER), w_a_bf)
    mb = matmul(y_b, w_b_bf)
    mixed = jax.nn.sigmoid(gate_a.reshape(n, D_MODEL)) * ma + jax.nn.sigmoid(gate_b.reshape(n, D_MODEL)) * mb
    x2 = x2 + matmul(mixed, w_out_bf)
    x2 = peer_block(x2, norm2_g, wq_bf, keys_bf, ut_bf, v_bf)
    h_out = h_new.reshape(bsz, SSM_HEADS, SSM_HEAD_DIM, D_STATE)
    return x2.reshape(bsz, t_len, D_MODEL), k, v, ik, h_out, conv_new


def kernel(x_prompt, x_sample, cache_k, cache_v, cache_idx_k, state_ssm, state_conv, page_table,
           norm1_g, w_in, conv_w, conv_b, dt_bias, a_log, d_skip, ssm_norm_g, q_norm_g, k_norm_g,
           w_branch_a, w_branch_b, w_out, norm2_g, peer_wq, peer_keys, peer_u, peer_v):
    bp, seq = x_prompt.shape[:2]
    bs, t_new = x_sample.shape[:2]
    past = page_table.shape[1] * PAGE_SIZE
    pos_p = jnp.arange(seq)
    pos_s = past + jnp.arange(t_new)
    yp, ys = x_prompt, x_sample
    outs_p, outs_s = [], []
    for l in range(DEPTH):
        w_in_pad = jnp.pad(w_in[l].astype(BF16), ((0, 0), (0, IN_WIDTH_PAD - IN_WIDTH)))
        lw = (norm1_g[l], w_in_pad, conv_w[l], conv_b[l], dt_bias[l], a_log[l], d_skip[l],
              ssm_norm_g[l], q_norm_g[l], k_norm_g[l], w_branch_a[l].astype(BF16),
              w_branch_b[l].astype(BF16), w_out[l].astype(BF16), norm2_g[l],
              peer_wq[l].astype(BF16),
              peer_keys[l].astype(BF16).reshape(2 * PEER_HEADS, N_KEYS, PEER_HALF),
              peer_u[l].T.astype(BF16), peer_v[l].astype(BF16))
        conv0 = jnp.zeros((bp, CONV_W - 1, CONV_DIM), F32)
        h0 = jnp.zeros((bp, SSM_GROUPS, SSM_HPG, SSM_HEAD_DIM, D_STATE), F32)
        yp, *rest = _layer(yp, pos_p, lw, conv0, h0, None)
        outs_p.append(rest)
        h_s = state_ssm[l].reshape(bs, SSM_GROUPS, SSM_HPG, SSM_HEAD_DIM, D_STATE)
        ys, *rest = _layer(ys, pos_s, lw, state_conv[l], h_s,
                           (cache_k[l], cache_v[l], cache_idx_k[l], page_table))
        outs_s.append(rest)
    stack = lambda outs, i: jnp.stack([o[i] for o in outs])
    return (yp, ys,
            stack(outs_p, 0), stack(outs_p, 1), stack(outs_p, 2), stack(outs_p, 3), stack(outs_p, 4),
            stack(outs_s, 0), stack(outs_s, 1), stack(outs_s, 2), stack(outs_s, 3), stack(outs_s, 4))
```

```python
import functools
import math

import jax
import jax.numpy as jnp
import numpy as np
from jax import lax
from jax.experimental import pallas as pl
from jax.experimental.pallas import tpu as pltpu

D_MODEL = 1024
DEPTH = 2
PAGE_SIZE = 128
D_INNER = 2048
SSM_HEAD_DIM = 64
SSM_HEADS = 32
SSM_GROUPS = 4
SSM_HPG = 8
D_STATE = 128
CONV_W = 4
CONV_DIM = D_INNER + 2 * SSM_GROUPS * D_STATE
SSD_CHUNK = 128
ATT_HEADS = 16
KV_HEADS = 4
HEAD_DIM = 64
ATT_REP = 4
ATT_WIDTH = 1024
KV_WIDTH = 256
IDX_HEADS = 8
IDX_DIM = 64
IDX_SCALE = (IDX_HEADS * IDX_DIM) ** -0.5
TOPK_MAX = 256
Q_BLOCK = 128
ROPE_THETA = 10000.0
PEER_HEADS = 8
N_KEYS = 128
N_EXPERTS = N_KEYS * N_KEYS
PEER_KEY_DIM = 256
PEER_HALF = 128
PEER_TOPK = 16
PEER_PAIRS = PEER_HEADS * PEER_TOPK
EPS = 1e-6
IN_SIZES = (D_INNER, CONV_DIM, SSM_HEADS, ATT_WIDTH, KV_WIDTH, KV_WIDTH,
            IDX_HEADS * IDX_DIM, IDX_DIM, IDX_HEADS, D_MODEL, D_MODEL)
IN_WIDTH = sum(IN_SIZES)
OFF_Z, OFF_XBC, OFF_Q, OFF_KV, OFF_IQ, OFF_GA, OFF_GB, OFF_SMALL = 0, 2048, 5120, 6144, 6656, 7168, 8192, 9216
SMALL_IK, SMALL_DT, SMALL_IW = 0, 64, 96
IN_WIDTH_PAD = 9728

F32 = jnp.float32
BF16 = jnp.bfloat16
I32 = jnp.int32
INT_MIN = np.int32(-2 ** 31)
VMEM_LIMIT = 56 * 1024 * 1024
LANES = 128
NT_DIMS = (((1,), (1,)), ((), ()))


def _pick_tile(n, candidates):
    for c in candidates:
        if n % c == 0:
            return c
    return n


def _mm_norm_kernel(x_ref, g_ref, w_ref, o_ref, xn_ref):
    @pl.when(pl.program_id(1) == 0)
    def _():
        x = x_ref[...]
        ms = jnp.mean(x * x, axis=-1, keepdims=True)
        xn_ref[...] = (x * lax.rsqrt(ms + EPS) * g_ref[...]).astype(BF16)

    o_ref[...] = jnp.dot(xn_ref[...], w_ref[...], preferred_element_type=F32)


def _mm_kernel(x_ref, w_ref, o_ref):
    o_ref[...] = jnp.dot(x_ref[...].astype(BF16), w_ref[...], preferred_element_type=F32)


def matmul(x, w):
    m, k = x.shape
    n = w.shape[1]
    tm = _pick_tile(m, (1024, 512, 256, 128))
    tn = _pick_tile(n, (512, 256, 128))
    return pl.pallas_call(
        _mm_kernel,
        out_shape=jax.ShapeDtypeStruct((m, n), F32),
        grid=(m // tm, n // tn),
        in_specs=[pl.BlockSpec((tm, k), lambda i, j: (i, 0)),
                  pl.BlockSpec((k, tn), lambda i, j: (0, j))],
        out_specs=pl.BlockSpec((tm, tn), lambda i, j: (i, j)),
        compiler_params=pltpu.CompilerParams(dimension_semantics=("parallel", "arbitrary"),
                                             vmem_limit_bytes=VMEM_LIMIT),
        name="matmul",
    )(x, w)


def norm_matmul(x, gain, w):
    m, k = x.shape
    n = w.shape[1]
    tm = _pick_tile(m, (1024, 512, 256, 128))
    tn = _pick_tile(n, (512, 256, 128))
    return pl.pallas_call(
        _mm_norm_kernel,
        out_shape=(jax.ShapeDtypeStruct((m, n), F32), jax.ShapeDtypeStruct((m, k), BF16)),
        grid=(m // tm, n // tn),
        in_specs=[pl.BlockSpec((tm, k), lambda i, j: (i, 0)),
                  pl.BlockSpec((1, k), lambda i, j: (0, 0)),
                  pl.BlockSpec((k, tn), lambda i, j: (0, j))],
        out_specs=(pl.BlockSpec((tm, tn), lambda i, j: (i, j)),
                   pl.BlockSpec((tm, k), lambda i, j: (i, 0))),
        compiler_params=pltpu.CompilerParams(dimension_semantics=("parallel", "arbitrary"),
                                             vmem_limit_bytes=VMEM_LIMIT),
        name="norm_matmul",
    )(x, gain.reshape(1, k), w)


def _top16_rows(s):
    r = s.shape[0]
    rows = lax.broadcasted_iota(I32, s.shape, 0)
    vals, idxs = [], []
    for _ in range(PEER_TOPK):
        m = jnp.max(s, axis=0, keepdims=True)
        i = jnp.min(jnp.where(s == m, rows, r), axis=0, keepdims=True)
        vals.append(m)
        idxs.append(i)
        s = jnp.where(rows == i, -jnp.inf, s)
    return jnp.concatenate(vals, axis=0), jnp.concatenate(idxs, axis=0)


def _peer_pairs(sv0, si0, sv1, si1):
    t = sv0.shape[1]
    row8 = lax.broadcasted_iota(I32, (8, t), 0)
    row16 = lax.broadcasted_iota(I32, (16, t), 0)
    vals = [sv0[0:1] + sv1]
    eids = [si0[0:1] * N_KEYS + si1]
    flat = [row16]
    for a in range(1, 8):
        nb = PEER_TOPK // (a + 1)
        v = sv0[a:a + 1] + sv1[0:8]
        vals.append(v if nb >= 8 else jnp.where(row8 < nb, v, -jnp.inf))
        eids.append(si0[a:a + 1] * N_KEYS + si1[0:8])
        flat.append(row8 + PEER_TOPK * a)
    vals.append(sv0[8:16] + sv1[0:1])
    eids.append(si0[8:16] * N_KEYS + si1[0:1])
    flat.append((row8 + 8) * PEER_TOPK)
    cand = jnp.concatenate(vals, axis=0)
    ce = jnp.concatenate(eids, axis=0)
    cf = jnp.concatenate(flat, axis=0)
    fv, fe = [], []
    for _ in range(PEER_TOPK):
        m = jnp.max(cand, axis=0, keepdims=True)
        pick = jnp.min(jnp.where(cand == m, cf, PEER_TOPK * PEER_TOPK), axis=0, keepdims=True)
        sel = cf == pick
        fe.append(jnp.max(jnp.where(sel, ce, -1), axis=0, keepdims=True))
        fv.append(m)
        cand = jnp.where(sel, -jnp.inf, cand)
    fv = jnp.concatenate(fv, axis=0)
    fe = jnp.concatenate(fe, axis=0)
    p = jnp.exp(fv - fv[0:1])
    return fe, p / jnp.sum(p, axis=0, keepdims=True)


def _peer_topk_kernel(q_ref, keys_ref, a_ref, b_ref, g_ref):
    q = q_ref[...].astype(BF16)
    es, gs = [], []
    for h in range(PEER_HEADS):
        tops = []
        for c in range(2):
            hc = 2 * h + c
            s = lax.dot_general(keys_ref[hc], q[:, hc * PEER_HALF:(hc + 1) * PEER_HALF], NT_DIMS,
                                preferred_element_type=F32)
            tops.append(_top16_rows(s))
        e, g = _peer_pairs(tops[0][0], tops[0][1], tops[1][0], tops[1][1])
        es.append(e)
        gs.append(g)
    e = jnp.concatenate(es, axis=0)
    g = jnp.concatenate(gs, axis=0)
    a_ref[...] = (e >> 7).astype(F32).T
    b_ref[...] = (e & (N_KEYS - 1)).astype(F32).T
    g_ref[...] = g.T


PEER_TOPK_TILE = 128


def peer_topk(q, keys_bf):
    n = q.shape[0]
    tt = PEER_TOPK_TILE
    spec = pl.BlockSpec((tt, PEER_PAIRS), lambda i: (i, 0))
    shp = jax.ShapeDtypeStruct((n, PEER_PAIRS), F32)
    return pl.pallas_call(
        _peer_topk_kernel,
        out_shape=(shp, shp, shp),
        grid=(n // tt,),
        in_specs=[pl.BlockSpec((tt, PEER_HEADS * PEER_KEY_DIM), lambda i: (i, 0)),
                  pl.BlockSpec((2 * PEER_HEADS, N_KEYS, PEER_HALF), lambda i: (0, 0, 0))],
        out_specs=(spec, spec, spec),
        compiler_params=pltpu.CompilerParams(dimension_semantics=("parallel",),
                                             vmem_limit_bytes=VMEM_LIMIT),
        name="peer_topk",
    )(q, keys_bf)


W_PITCH = 136
PEER_W_TILE = 64


def _peer_w_kernel(a_ref, b_ref, g_ref, o_ref, wf_ref):
    tt = a_ref.shape[0]
    ii = lax.broadcasted_iota(I32, (N_KEYS, PEER_PAIRS), 0).astype(F32)

    def body(t, carry):
        a = a_ref[pl.ds(t, 1), :]
        b = b_ref[pl.ds(t, 1), :]
        g = g_ref[pl.ds(t, 1), :]
        at = jnp.where(ii == a, 1.0, 0.0).astype(BF16)
        bt = jnp.where(ii == b, g, 0.0).astype(BF16)
        w = lax.dot_general(at, bt, NT_DIMS, preferred_element_type=F32)
        wf_ref[pl.ds(pl.multiple_of(t * W_PITCH, 8), N_KEYS), :] = w
        return carry

    lax.fori_loop(0, tt, body, 0, unroll=8)
    for i1 in range(N_KEYS):
        o_ref[:, i1 * N_KEYS:(i1 + 1) * N_KEYS] = wf_ref[pl.ds(i1, tt, stride=W_PITCH), :].astype(BF16)


def peer_gate_matrix(a, b, g):
    n = a.shape[0]
    tt = PEER_W_TILE
    spec = pl.BlockSpec((tt, PEER_PAIRS), lambda i: (i, 0))
    return pl.pallas_call(
        _peer_w_kernel,
        out_shape=jax.ShapeDtypeStruct((n, N_EXPERTS), BF16),
        grid=(n // tt,),
        in_specs=[spec, spec, spec],
        out_specs=pl.BlockSpec((tt, N_EXPERTS), lambda i: (i, 0)),
        scratch_shapes=[pltpu.VMEM((tt * W_PITCH, N_KEYS), F32)],
        compiler_params=pltpu.CompilerParams(dimension_semantics=("parallel",),
                                             vmem_limit_bytes=VMEM_LIMIT),
        name="peer_gate_matrix",
    )(a, b, g)


def _peer_dense_kernel(xn_ref, x_ref, ut_ref, w_ref, v_ref, o_ref):
    @pl.when(pl.program_id(1) == 0)
    def _():
        o_ref[...] = x_ref[...]

    h = jnp.dot(xn_ref[...], ut_ref[...], preferred_element_type=F32)
    act = 0.5 * h * (1.0 + lax.erf(h * np.float32(math.sqrt(0.5))))
    wa = (w_ref[...].astype(F32) * act).astype(BF16)
    o_ref[...] += jnp.dot(wa, v_ref[...], preferred_element_type=F32)


PEER_EXPERT_TILE = 1024


def peer_dense(xn, x, ut, w, v):
    n = x.shape[0]
    tm = _pick_tile(n, (1024, 512, 256, 128))
    te = PEER_EXPERT_TILE
    return pl.pallas_call(
        _peer_dense_kernel,
        out_shape=jax.ShapeDtypeStruct((n, D_MODEL), F32),
        grid=(n // tm, N_EXPERTS // te),
        in_specs=[pl.BlockSpec((tm, D_MODEL), lambda i, j: (i, 0)),
                  pl.BlockSpec((tm, D_MODEL), lambda i, j: (i, 0)),
                  pl.BlockSpec((D_MODEL, te), lambda i, j: (0, j)),
                  pl.BlockSpec((tm, te), lambda i, j: (i, j)),
                  pl.BlockSpec((te, D_MODEL), lambda i, j: (j, 0))],
        out_specs=pl.BlockSpec((tm, D_MODEL), lambda i, j: (i, 0)),
        compiler_params=pltpu.CompilerParams(dimension_semantics=("parallel", "arbitrary"),
                                             vmem_limit_bytes=VMEM_LIMIT),
        name="peer_dense",
    )(xn, x, ut, w, v)


def peer_block(x2, norm2_g, wq_bf, keys_bf, ut_bf, v_bf):
    n = x2.shape[0]
    npad = -(-n // PEER_TOPK_TILE) * PEER_TOPK_TILE
    xp = jnp.pad(x2, ((0, npad - n), (0, 0))) if npad != n else x2
    q, xn = norm_matmul(xp, norm2_g, wq_bf)
    a, b, g = peer_topk(q, keys_bf)
    w = peer_gate_matrix(a, b, g)
    out = peer_dense(xn, xp, ut_bf, w, v_bf)
    return out[:n] if npad != n else out


def _group_sum64(xx, bd):
    hi = xx.astype(BF16)
    r1 = xx - hi.astype(F32)
    mid = r1.astype(BF16)
    lo = (r1 - mid.astype(F32)).astype(BF16)
    dot = lambda a: jnp.dot(a, bd, preferred_element_type=F32)
    return dot(hi) + dot(mid) + dot(lo)


def _rope128(y, c, s, lo_half):
    partner = jnp.where(lo_half, pltpu.roll(y, LANES - 32, axis=1), pltpu.roll(y, 32, axis=1))
    return y * c + partner * s


def _attn_prep_kernel(q_ref, kv_ref, iq_ref, sm_ref, cos_ref, sin_ref, qg_ref, kg_ref,
                      qo_ref, ko_ref, vo_ref, iqo_ref, iko_ref):
    tm = q_ref.shape[0]
    lane = lax.broadcasted_iota(I32, (tm, LANES), 1)
    lo_half = (lane & (HEAD_DIM - 1)) < HEAD_DIM // 2
    grp_r = lax.broadcasted_iota(I32, (LANES, LANES), 0) >> 6
    grp_c = lax.broadcasted_iota(I32, (LANES, LANES), 1) >> 6
    bd = jnp.where(grp_r == grp_c, 1.0, 0.0).astype(BF16)
    c = cos_ref[...]
    s = sin_ref[...]

    def normed(x, g):
        ms = _group_sum64(x * x, bd) * np.float32(1.0 / HEAD_DIM)
        return x * lax.rsqrt(ms + EPS) * g

    for j in range(ATT_WIDTH // LANES):
        sl = slice(j * LANES, (j + 1) * LANES)
        qo_ref[:, sl] = _rope128(normed(q_ref[:, sl], qg_ref[...]), c, s, lo_half)
    for j in range(KV_WIDTH // LANES):
        sl = slice(j * LANES, (j + 1) * LANES)
        ko_ref[:, sl] = _rope128(normed(kv_ref[:, sl], kg_ref[...]), c, s, lo_half)
    vo_ref[...] = kv_ref[:, KV_WIDTH:2 * KV_WIDTH]
    for j in range(IDX_HEADS * IDX_DIM // LANES):
        sl = slice(j * LANES, (j + 1) * LANES)
        iqo_ref[:, sl] = _rope128(iq_ref[:, sl], c, s, lo_half)
    iko_ref[...] = _rope128(sm_ref[...], c, s, lo_half)[:, SMALL_IK:SMALL_IK + IDX_DIM]


def attn_prep(proj, cos_t, sin_t, q_norm_g, k_norm_g):
    n = proj.shape[0]
    tm = _pick_tile(n, (256, 128))
    nt = cos_t.shape[0] // tm
    row = lambda w, off: pl.BlockSpec((tm, w), lambda i: (i, off // w))
    tab = pl.BlockSpec((tm, LANES), lambda i: (i % nt, 0))
    gain = pl.BlockSpec((1, LANES), lambda i: (0, 0))
    out = lambda w: pl.BlockSpec((tm, w), lambda i: (i, 0))
    shp = lambda w: jax.ShapeDtypeStruct((n, w), F32)
    tile2 = lambda g: jnp.tile(g.reshape(1, HEAD_DIM), (1, LANES // HEAD_DIM))
    return pl.pallas_call(
        _attn_prep_kernel,
        out_shape=(shp(ATT_WIDTH), shp(KV_WIDTH), shp(KV_WIDTH), shp(IDX_HEADS * IDX_DIM), shp(IDX_DIM)),
        grid=(n // tm,),
        in_specs=[row(ATT_WIDTH, OFF_Q), row(2 * KV_WIDTH, OFF_KV), row(IDX_HEADS * IDX_DIM, OFF_IQ),
                  row(LANES, OFF_SMALL), tab, tab, gain, gain],
        out_specs=(out(ATT_WIDTH), out(KV_WIDTH), out(KV_WIDTH), out(IDX_HEADS * IDX_DIM), out(IDX_DIM)),
        compiler_params=pltpu.CompilerParams(dimension_semantics=("parallel",),
                                             vmem_limit_bytes=VMEM_LIMIT),
        name="attn_prep",
    )(proj, proj, proj, proj, cos_t, sin_t, tile2(q_norm_g), tile2(k_norm_g))


def rope_tables(pos):
    inv = ROPE_THETA ** (-jnp.arange(0, HEAD_DIM, 2, dtype=F32) / HEAD_DIM)
    ang = pos.astype(F32)[:, None] * inv[None, :]
    cos, sin = jnp.cos(ang), jnp.sin(ang)
    reps = LANES // HEAD_DIM
    return (jnp.tile(jnp.concatenate([cos, cos], axis=-1), (1, reps)),
            jnp.tile(jnp.concatenate([-sin, sin], axis=-1), (1, reps)))


def _mix_kernel(ya_ref, yb_ref, ga_ref, gb_ref, x_ref, wa_ref, wb_ref, wo_ref, o_ref):
    ma = jnp.dot(ya_ref[...].astype(BF16), wa_ref[...], preferred_element_type=F32)
    mb = jnp.dot(yb_ref[...].astype(BF16), wb_ref[...], preferred_element_type=F32)
    mixed = jax.nn.sigmoid(ga_ref[...]) * ma + jax.nn.sigmoid(gb_ref[...]) * mb
    o_ref[...] = x_ref[...] + jnp.dot(mixed.astype(BF16), wo_ref[...], preferred_element_type=F32)


def branch_mix(ya, yb, proj, x, wa, wb, wo):
    n = x.shape[0]
    tm = _pick_tile(n, (256, 128))
    rows = lambda w, blk=0: pl.BlockSpec((tm, w), lambda i: (i, blk))
    whole = lambda a: pl.BlockSpec(a.shape, lambda i: (0, 0))
    return pl.pallas_call(
        _mix_kernel,
        out_shape=jax.ShapeDtypeStruct((n, D_MODEL), F32),
        grid=(n // tm,),
        in_specs=[rows(D_INNER), rows(ATT_WIDTH), rows(D_MODEL, OFF_GA // D_MODEL),
                  rows(D_MODEL, OFF_GB // D_MODEL), rows(D_MODEL), whole(wa), whole(wb), whole(wo)],
        out_specs=rows(D_MODEL),
        compiler_params=pltpu.CompilerParams(dimension_semantics=("parallel",),
                                             vmem_limit_bytes=VMEM_LIMIT),
        name="branch_mix",
    )(ya, yb, proj, proj, x, wa, wb, wo)


def _dsa_prompt_body(s_len, qb, iq_ref, sm_ref, ik_ref, q_ref, k_ref, v_ref, o_ref, jc_ref):
    ik = ik_ref[0:s_len, :].astype(BF16)
    iw = sm_ref[:, SMALL_IW:SMALL_IW + IDX_HEADS] * np.float32(IDX_SCALE)
    score = jnp.zeros((Q_BLOCK, s_len), F32)
    for h in range(IDX_HEADS):
        iqh = iq_ref[:, h * IDX_DIM:(h + 1) * IDX_DIM].astype(BF16)
        d = lax.dot_general(iqh, ik, NT_DIMS, preferred_element_type=F32)
        score = score + jnp.maximum(d, 0.0) * iw[:, h:h + 1]

    tpos = qb * Q_BLOCK + lax.broadcasted_iota(I32, (Q_BLOCK, 1), 0)
    spos = lax.broadcasted_iota(I32, (1, s_len), 1)
    causal = spos <= tpos
    bits = lax.bitcast_convert_type(score, I32)
    key = bits ^ ((bits >> 31) & np.int32(0x7FFFFFFF))
    key = jnp.where(score == 0.0, 0, key)
    key = jnp.where(causal, key, INT_MIN)

    def count(mask):
        return jnp.sum(mask.astype(I32), axis=-1, keepdims=True)

    def bisect_value(i, tu):
        cand_u = tu | jnp.left_shift(np.int32(1), 31 - i)
        return jnp.where(count(key >= (cand_u ^ INT_MIN)) >= TOPK_MAX, cand_u, tu)

    thr = lax.fori_loop(0, 32, bisect_value, jnp.zeros((Q_BLOCK, 1), I32)) ^ INT_MIN
    above = key > thr
    tied = key == thr
    need = TOPK_MAX - count(above)
    excess = (count(tied) > need) & (thr != INT_MIN)

    jc_ref[...] = jnp.full((Q_BLOCK, LANES), s_len, I32)

    @pl.when(jnp.max(excess.astype(I32)) > 0)
    def _():
        def bisect_pos(i, j):
            cj = j | jnp.left_shift(np.int32(1), 10 - i)
            return jnp.where(count(tied & (spos < cj)) < need, cj, j)

        jc = lax.fori_loop(0, 11, bisect_pos, jnp.zeros((Q_BLOCK, 1), I32))
        jc_ref[...] = jnp.broadcast_to(jc, (Q_BLOCK, LANES))

    jc = jc_ref[:, 0:1]
    sel = causal & (above | (tied & (spos <= jc)))
    bias = jnp.where(sel, 0.0, -jnp.inf)

    for g in range(KV_HEADS):
        qs = jnp.concatenate(
            [q_ref[:, (ATT_REP * g + r) * HEAD_DIM:(ATT_REP * g + r + 1) * HEAD_DIM] for r in range(ATT_REP)],
            axis=0).astype(BF16)
        kg = k_ref[0:s_len, g * HEAD_DIM:(g + 1) * HEAD_DIM].astype(BF16)
        vg = v_ref[0:s_len, g * HEAD_DIM:(g + 1) * HEAD_DIM].astype(BF16)
        s = lax.dot_general(qs, kg, NT_DIMS, preferred_element_type=F32) * np.float32(HEAD_DIM ** -0.5)
        s = (s.reshape(ATT_REP, Q_BLOCK, s_len) + bias[None]).reshape(ATT_REP * Q_BLOCK, s_len)
        m = jnp.max(s, axis=-1, keepdims=True)
        p = jnp.exp(s - m)
        l = jnp.sum(p, axis=-1, keepdims=True)
        o = jnp.dot(p.astype(BF16), vg, preferred_element_type=F32) / l
        for r in range(ATT_REP):
            hh = ATT_REP * g + r
            o_ref[:, hh * HEAD_DIM:(hh + 1) * HEAD_DIM] = o[r * Q_BLOCK:(r + 1) * Q_BLOCK]


DSA_KEY_BUCKET = 512


def _dsa_prompt_kernel(*refs):
    qb = pl.program_id(1)
    s_full = refs[2].shape[0]
    per = DSA_KEY_BUCKET // Q_BLOCK
    for j in range(s_full // DSA_KEY_BUCKET):
        pl.when((qb >= per * j) & (qb < per * (j + 1)))(
            functools.partial(_dsa_prompt_body, DSA_KEY_BUCKET * (j + 1), qb, *refs))


def dsa_prompt(q, k, v, iq, ik, proj, bsz, seq):
    nq = seq // Q_BLOCK
    assert seq == 2048 and TOPK_MAX <= seq // 4
    qmap = lambda b, i: (b * nq + i, 0)
    bmap = lambda b, i: (b, 0)
    return pl.pallas_call(
        _dsa_prompt_kernel,
        out_shape=jax.ShapeDtypeStruct((bsz * seq, ATT_WIDTH), F32),
        grid=(bsz, nq),
        in_specs=[pl.BlockSpec((Q_BLOCK, IDX_HEADS * IDX_DIM), qmap),
                  pl.BlockSpec((Q_BLOCK, LANES), lambda b, i: (b * nq + i, OFF_SMALL // LANES)),
                  pl.BlockSpec((seq, IDX_DIM), bmap),
                  pl.BlockSpec((Q_BLOCK, ATT_WIDTH), qmap),
                  pl.BlockSpec((seq, KV_WIDTH), bmap),
                  pl.BlockSpec((seq, KV_WIDTH), bmap)],
        out_specs=pl.BlockSpec((Q_BLOCK, ATT_WIDTH), qmap),
        scratch_shapes=[pltpu.VMEM((Q_BLOCK, LANES), I32)],
        compiler_params=pltpu.CompilerParams(dimension_semantics=("parallel", "arbitrary"),
                                             vmem_limit_bytes=VMEM_LIMIT),
        name="dsa_prompt",
    )(iq, proj, ik, q, k, v)


def _rms(x, g):
    xf = x.astype(F32)
    y = xf * lax.rsqrt(jnp.mean(xf * xf, axis=-1, keepdims=True) + EPS)
    return y * g.astype(F32)


def _rope(x, pos):
    d = x.shape[-1]
    inv = ROPE_THETA ** (-jnp.arange(0, d, 2, dtype=F32) / d)
    ang = pos.astype(F32)[:, None] * inv[None, :]
    ang = ang.reshape((ang.shape[0],) + (1,) * (x.ndim - 3) + (d // 2,))
    cos, sin = jnp.cos(ang), jnp.sin(ang)
    x1, x2 = x[..., :d // 2], x[..., d // 2:]
    return jnp.concatenate([x1 * cos - x2 * sin, x2 * cos + x1 * sin], axis=-1)


def _gather_rows(rows, idx):
    return jax.vmap(lambda r, i: r[i])(rows, idx)


def _ssd_chunked(xs, dt, a, bm, cm, h0):
    bsz, seq = xs.shape[:2]
    nc = seq // SSD_CHUNK

    def to_chunks(t):
        return jnp.moveaxis(t.reshape((bsz, nc, SSD_CHUNK) + t.shape[2:]), 1, 0)

    xd = xs * dt[..., None]
    ad = dt * a
    causal = jnp.tril(jnp.ones((SSD_CHUNK, SSD_CHUNK), dtype=bool))

    def step(h, inp):
        xd_c, ad_c, b_c, c_c = inp
        acs = jnp.cumsum(ad_c, axis=1)
        seg = acs[:, :, None] - acs[:, None, :]
        lmat = jnp.exp(jnp.where(causal[None, :, :, None, None], seg, -jnp.inf))
        cb = jnp.einsum('blgn,bsgn->blsg', c_c, b_c)
        y_diag = jnp.einsum('blsg,blsgr,bsgrp->blgrp', cb, lmat, xd_c)
        y_off = jnp.einsum('blgn,bgrpn,blgr->blgrp', c_c, h, jnp.exp(acs))
        decay_end = jnp.exp(acs[:, -1:] - acs)
        h_new = h * jnp.exp(acs[:, -1])[..., None, None] + jnp.einsum('bsgn,bsgr,bsgrp->bgrpn', b_c, decay_end, xd_c)
        return h_new, y_diag + y_off

    h, ys = lax.scan(step, h0, (to_chunks(xd), to_chunks(ad), to_chunks(bm), to_chunks(cm)))
    y = jnp.moveaxis(ys, 0, 1).reshape(xs.shape)
    return y, h


def _ssd_recurrent(xs, dt, a, bm, cm, h0):
    def step(h, inp):
        x_t, dt_t, b_t, c_t = inp
        h = h * jnp.exp(dt_t * a)[..., None, None] + jnp.einsum('bgrp,bgn->bgrpn', x_t * dt_t[..., None], b_t)
        return h, jnp.einsum('bgn,bgrpn->bgrp', c_t, h)

    tf = lambda t: jnp.moveaxis(t, 1, 0)
    h, ys = lax.scan(step, h0, (tf(xs), tf(dt), tf(bm), tf(cm)))
    return jnp.moveaxis(ys, 0, 1), h


def _mamba2(z, xbc, dt_raw, conv_prefix, h0, conv_w, conv_b, dt_bias, a_log, d_skip, norm_g, chunked):
    bsz, t_len, _ = z.shape
    xp = jnp.concatenate([conv_prefix, xbc], axis=1)
    conv = conv_b + sum(xp[:, j:j + t_len] * conv_w[j] for j in range(CONV_W))
    conv_new = xp[:, t_len:]
    xbc = jax.nn.silu(conv)
    xs, bm, cm = jnp.split(xbc, [D_INNER, D_INNER + SSM_GROUPS * D_STATE], axis=-1)
    xs = xs.reshape(bsz, t_len, SSM_GROUPS, SSM_HPG, SSM_HEAD_DIM)
    bm = bm.reshape(bsz, t_len, SSM_GROUPS, D_STATE)
    cm = cm.reshape(bsz, t_len, SSM_GROUPS, D_STATE)
    dt = jax.nn.softplus(dt_raw + dt_bias).reshape(bsz, t_len, SSM_GROUPS, SSM_HPG)
    a = -jnp.exp(a_log).reshape(SSM_GROUPS, SSM_HPG)
    scan_fn = _ssd_chunked if chunked else _ssd_recurrent
    y, h = scan_fn(xs, dt, a, bm, cm, h0)
    y = y + d_skip.reshape(SSM_GROUPS, SSM_HPG, 1) * xs
    y = y.reshape(bsz, t_len, D_INNER) * jax.nn.silu(z)
    y = _rms(y.reshape(bsz, t_len, SSM_GROUPS, D_INNER // SSM_GROUPS), norm_g.reshape(SSM_GROUPS, -1))
    return y.reshape(bsz, t_len, D_INNER), h, conv_new


def _indexer_scores(iq, ik, iw):
    dots = jax.nn.relu(jnp.einsum('bthd,bsd->bths', iq, ik))
    return jnp.einsum('bths,bth->bts', dots, iw * IDX_SCALE)


def _sparse_attend(q, ks, vs, valid):
    bsz, t_len = q.shape[:2]
    qg = q.reshape(bsz, t_len, KV_HEADS, ATT_REP, HEAD_DIM)
    s = jnp.einsum('btgrd,btkgd->btgrk', qg, ks) * (HEAD_DIM ** -0.5)
    s = jnp.where(valid[:, :, None, None, :], s, -jnp.inf)
    p = jax.nn.softmax(s, axis=-1)
    o = jnp.einsum('btgrk,btkgd->btgrd', p, vs)
    return o.reshape(bsz, t_len, ATT_WIDTH)


def _dsa_sample(q, k, v, iq, ik, iw, ck, cv, cik, page_table):
    bsz, t_len = q.shape[:2]
    past = page_table.shape[1] * PAGE_SIZE
    n_keys = past + t_len
    topk = min(TOPK_MAX, n_keys // 4)
    ik_past = cik[page_table].reshape(bsz, past, IDX_DIM)
    score = _indexer_scores(iq, jnp.concatenate([ik_past, ik], axis=1), iw)
    qpos = past + jnp.arange(t_len)
    allowed = jnp.arange(n_keys)[None, :] <= qpos[:, None]
    score = jnp.where(allowed[None], score, -jnp.inf)
    _, idx = lax.top_k(score, topk)
    valid = idx <= qpos[None, :, None]
    from_new = idx >= past
    p_idx = jnp.minimum(idx, past - 1)
    phys = jax.vmap(lambda pt, i: pt[i // PAGE_SIZE])(page_table, p_idx)
    off = p_idx % PAGE_SIZE
    n_idx = jnp.clip(idx - past, 0, t_len - 1)

    def select(cache, new):
        return jnp.where(from_new[..., None, None], _gather_rows(new, n_idx), cache[phys, off])

    return _sparse_attend(q, select(ck, k), select(cv, v), valid)


def _layer(x, pos, lw, conv_prefix, h0, paged):
    (norm1_g, w_in_bf, conv_w, conv_b, dt_bias, a_log, d_skip, ssm_norm_g, q_norm_g, k_norm_g,
     w_a_bf, w_b_bf, w_out_bf, norm2_g, wq_bf, keys_bf, ut_bf, v_bf) = lw
    bsz, t_len, _ = x.shape
    n = bsz * t_len
    x2 = x.reshape(n, D_MODEL)
    proj, _ = norm_matmul(x2, norm1_g, w_in_bf)
    cos_t, sin_t = rope_tables(pos)
    q, k, v, iq, ik = attn_prep(proj, cos_t, sin_t, q_norm_g, k_norm_g)
    seg = lambda off, w: proj[:, off:off + w].reshape(bsz, t_len, w)
    y_a, h_new, conv_new = _mamba2(seg(OFF_Z, D_INNER), seg(OFF_XBC, CONV_DIM),
                                   seg(OFF_SMALL + SMALL_DT, SSM_HEADS), conv_prefix, h0, conv_w, conv_b,
                                   dt_bias, a_log, d_skip, ssm_norm_g, paged is None)
    k4 = k.reshape(bsz, t_len, KV_HEADS, HEAD_DIM)
    v4 = v.reshape(bsz, t_len, KV_HEADS, HEAD_DIM)
    ik3 = ik.reshape(bsz, t_len, IDX_DIM)
    if paged is None:
        y_b = dsa_prompt(q, k, v, iq, ik, proj, bsz, t_len)
    else:
        y_b = _dsa_sample(q.reshape(bsz, t_len, ATT_HEADS, HEAD_DIM), k4, v4,
                          iq.reshape(bsz, t_len, IDX_HEADS, IDX_DIM), ik3,
                          seg(OFF_SMALL + SMALL_IW, IDX_HEADS), *paged).reshape(n, ATT_WIDTH)
    x2 = branch_mix(y_a.reshape(n, D_INNER), y_b, proj, x2, w_a_bf, w_b_bf, w_out_bf)
    x2 = peer_block(x2, norm2_g, wq_bf, keys_bf, ut_bf, v_bf)
    h_out = h_new.reshape(bsz, SSM_HEADS, SSM_HEAD_DIM, D_STATE)
    return x2.reshape(bsz, t_len, D_MODEL), k4, v4, ik3, h_out, conv_new


def kernel(x_prompt, x_sample, cache_k, cache_v, cache_idx_k, state_ssm, state_conv, page_table,
           norm1_g, w_in, conv_w, conv_b, dt_bias, a_log, d_skip, ssm_norm_g, q_norm_g, k_norm_g,
           w_branch_a, w_branch_b, w_out, norm2_g, peer_wq, peer_keys, peer_u, peer_v):
    bp, seq = x_prompt.shape[:2]
    bs, t_new = x_sample.shape[:2]
    past = page_table.shape[1] * PAGE_SIZE
    pos_p = jnp.arange(seq)
    pos_s = jnp.tile(past + jnp.arange(t_new), bs)
    yp, ys = x_prompt, x_sample
    outs_p, outs_s = [], []
    sizes = dict(zip(("z", "xbc", "dt", "q", "k", "v", "iq", "ik", "iw", "ga", "gb"), IN_SIZES))
    starts = dict(zip(sizes, np.cumsum((0,) + IN_SIZES[:-1])))
    order = ("z", "xbc", "q", "k", "v", "iq", "ga", "gb", "ik", "dt", "iw")
    for l in range(DEPTH):
        w_bf = w_in[l].astype(BF16)
        cols = [w_bf[:, int(starts[s]):int(starts[s]) + sizes[s]] for s in order]
        used = sum(sizes.values())
        w_in_pad = jnp.concatenate(cols + [jnp.zeros((D_MODEL, IN_WIDTH_PAD - used), BF16)], axis=1)
        lw = (norm1_g[l], w_in_pad, conv_w[l], conv_b[l], dt_bias[l], a_log[l], d_skip[l],
              ssm_norm_g[l], q_norm_g[l], k_norm_g[l], w_branch_a[l].astype(BF16),
              w_branch_b[l].astype(BF16), w_out[l].astype(BF16), norm2_g[l],
              peer_wq[l].astype(BF16),
              peer_keys[l].astype(BF16).reshape(2 * PEER_HEADS, N_KEYS, PEER_HALF),
              peer_u[l].T.astype(BF16), peer_v[l].astype(BF16))
        conv0 = jnp.zeros((bp, CONV_W - 1, CONV_DIM), F32)
        h0 = jnp.zeros((bp, SSM_GROUPS, SSM_HPG, SSM_HEAD_DIM, D_STATE), F32)
        yp, *rest = _layer(yp, pos_p, lw, conv0, h0, None)
        outs_p.append(rest)
        h_s = state_ssm[l].reshape(bs, SSM_GROUPS, SSM_HPG, SSM_HEAD_DIM, D_STATE)
        ys, *rest = _layer(ys, pos_s, lw, state_conv[l], h_s,
                           (cache_k[l], cache_v[l], cache_idx_k[l], page_table))
        outs_s.append(rest)
    stack = lambda outs, i: jnp.stack([o[i] for o in outs])
    return (yp, ys,
            stack(outs_p, 0), stack(outs_p, 1), stack(outs_p, 2), stack(outs_p, 3), stack(outs_p, 4),
            stack(outs_s, 0), stack(outs_s, 1), stack(outs_s, 2), stack(outs_s, 3), stack(outs_s, 4))
```

```python
import functools
import math

import jax
import jax.numpy as jnp
import numpy as np
from jax import lax
from jax.experimental import pallas as pl
from jax.experimental.pallas import tpu as pltpu

D_MODEL = 1024
DEPTH = 2
PAGE_SIZE = 128
D_INNER = 2048
SSM_HEAD_DIM = 64
SSM_HEADS = 32
SSM_GROUPS = 4
SSM_HPG = 8
D_STATE = 128
CONV_W = 4
CONV_DIM = D_INNER + 2 * SSM_GROUPS * D_STATE
SSD_CHUNK = 128
ATT_HEADS = 16
KV_HEADS = 4
HEAD_DIM = 64
ATT_REP = 4
ATT_WIDTH = 1024
KV_WIDTH = 256
IDX_HEADS = 8
IDX_DIM = 64
IDX_SCALE = (IDX_HEADS * IDX_DIM) ** -0.5
TOPK_MAX = 256
Q_BLOCK = 128
ROPE_THETA = 10000.0
PEER_HEADS = 8
N_KEYS = 128
N_EXPERTS = N_KEYS * N_KEYS
PEER_KEY_DIM = 256
PEER_HALF = 128
PEER_TOPK = 16
PEER_PAIRS = PEER_HEADS * PEER_TOPK
EPS = 1e-6
IN_SIZES = (D_INNER, CONV_DIM, SSM_HEADS, ATT_WIDTH, KV_WIDTH, KV_WIDTH,
            IDX_HEADS * IDX_DIM, IDX_DIM, IDX_HEADS, D_MODEL, D_MODEL)
IN_WIDTH = sum(IN_SIZES)
OFF_Z, OFF_XBC, OFF_Q, OFF_KV, OFF_IQ, OFF_GA, OFF_GB, OFF_SMALL = 0, 2048, 5120, 6144, 6656, 7168, 8192, 9216
SMALL_IK, SMALL_DT, SMALL_IW = 0, 64, 96
IN_WIDTH_PAD = 9728

F32 = jnp.float32
BF16 = jnp.bfloat16
I32 = jnp.int32
INT_MIN = np.int32(-2 ** 31)
VMEM_LIMIT = 56 * 1024 * 1024
LANES = 128
NT_DIMS = (((1,), (1,)), ((), ()))


def _pick_tile(n, candidates):
    for c in candidates:
        if n % c == 0:
            return c
    return n


def _mm_norm_kernel(x_ref, g_ref, w_ref, o_ref, xn_ref):
    @pl.when(pl.program_id(1) == 0)
    def _():
        x = x_ref[...]
        ms = jnp.mean(x * x, axis=-1, keepdims=True)
        xn_ref[...] = (x * lax.rsqrt(ms + EPS) * g_ref[...]).astype(BF16)

    o_ref[...] = jnp.dot(xn_ref[...], w_ref[...], preferred_element_type=F32)


def _mm_kernel(x_ref, w_ref, o_ref):
    o_ref[...] = jnp.dot(x_ref[...].astype(BF16), w_ref[...], preferred_element_type=F32)


def matmul(x, w):
    m, k = x.shape
    n = w.shape[1]
    tm = _pick_tile(m, (1024, 512, 256, 128))
    tn = _pick_tile(n, (512, 256, 128))
    return pl.pallas_call(
        _mm_kernel,
        out_shape=jax.ShapeDtypeStruct((m, n), F32),
        grid=(m // tm, n // tn),
        in_specs=[pl.BlockSpec((tm, k), lambda i, j: (i, 0)),
                  pl.BlockSpec((k, tn), lambda i, j: (0, j))],
        out_specs=pl.BlockSpec((tm, tn), lambda i, j: (i, j)),
        compiler_params=pltpu.CompilerParams(dimension_semantics=("parallel", "arbitrary"),
                                             vmem_limit_bytes=VMEM_LIMIT),
        name="matmul",
    )(x, w)


def norm_matmul(x, gain, w):
    m, k = x.shape
    n = w.shape[1]
    tm = _pick_tile(m, (1024, 512, 256, 128))
    tn = _pick_tile(n, (512, 256, 128))
    return pl.pallas_call(
        _mm_norm_kernel,
        out_shape=(jax.ShapeDtypeStruct((m, n), F32), jax.ShapeDtypeStruct((m, k), BF16)),
        grid=(m // tm, n // tn),
        in_specs=[pl.BlockSpec((tm, k), lambda i, j: (i, 0)),
                  pl.BlockSpec((1, k), lambda i, j: (0, 0)),
                  pl.BlockSpec((k, tn), lambda i, j: (0, j))],
        out_specs=(pl.BlockSpec((tm, tn), lambda i, j: (i, j)),
                   pl.BlockSpec((tm, k), lambda i, j: (i, 0))),
        compiler_params=pltpu.CompilerParams(dimension_semantics=("parallel", "arbitrary"),
                                             vmem_limit_bytes=VMEM_LIMIT),
        name="norm_matmul",
    )(x, gain.reshape(1, k), w)


def _top16_rows(s):
    r = s.shape[0]
    rows = lax.broadcasted_iota(I32, s.shape, 0)
    vals, idxs = [], []
    for _ in range(PEER_TOPK):
        m = jnp.max(s, axis=0, keepdims=True)
        i = jnp.min(jnp.where(s == m, rows, r), axis=0, keepdims=True)
        vals.append(m)
        idxs.append(i)
        s = jnp.where(rows == i, -jnp.inf, s)
    return jnp.concatenate(vals, axis=0), jnp.concatenate(idxs, axis=0)


def _peer_pairs(sv0, si0, sv1, si1):
    t = sv0.shape[1]
    row8 = lax.broadcasted_iota(I32, (8, t), 0)
    row16 = lax.broadcasted_iota(I32, (16, t), 0)
    vals = [sv0[0:1] + sv1]
    eids = [si0[0:1] * N_KEYS + si1]
    flat = [row16]
    for a in range(1, 8):
        nb = PEER_TOPK // (a + 1)
        v = sv0[a:a + 1] + sv1[0:8]
        vals.append(v if nb >= 8 else jnp.where(row8 < nb, v, -jnp.inf))
        eids.append(si0[a:a + 1] * N_KEYS + si1[0:8])
        flat.append(row8 + PEER_TOPK * a)
    vals.append(sv0[8:16] + sv1[0:1])
    eids.append(si0[8:16] * N_KEYS + si1[0:1])
    flat.append((row8 + 8) * PEER_TOPK)
    cand = jnp.concatenate(vals, axis=0)
    ce = jnp.concatenate(eids, axis=0)
    cf = jnp.concatenate(flat, axis=0)
    fv, fe = [], []
    for _ in range(PEER_TOPK):
        m = jnp.max(cand, axis=0, keepdims=True)
        pick = jnp.min(jnp.where(cand == m, cf, PEER_TOPK * PEER_TOPK), axis=0, keepdims=True)
        sel = cf == pick
        fe.append(jnp.max(jnp.where(sel, ce, -1), axis=0, keepdims=True))
        fv.append(m)
        cand = jnp.where(sel, -jnp.inf, cand)
    fv = jnp.concatenate(fv, axis=0)
    fe = jnp.concatenate(fe, axis=0)
    p = jnp.exp(fv - fv[0:1])
    return fe, p / jnp.sum(p, axis=0, keepdims=True)


def _peer_topk_kernel(q_ref, keys_ref, a_ref, b_ref, g_ref):
    q = q_ref[...].astype(BF16)
    es, gs = [], []
    for h in range(PEER_HEADS):
        tops = []
        for c in range(2):
            hc = 2 * h + c
            s = lax.dot_general(keys_ref[hc], q[:, hc * PEER_HALF:(hc + 1) * PEER_HALF], NT_DIMS,
                                preferred_element_type=F32)
            tops.append(_top16_rows(s))
        e, g = _peer_pairs(tops[0][0], tops[0][1], tops[1][0], tops[1][1])
        es.append(e)
        gs.append(g)
    e = jnp.concatenate(es, axis=0)
    g = jnp.concatenate(gs, axis=0)
    a_ref[...] = (e >> 7).astype(F32).T
    b_ref[...] = (e & (N_KEYS - 1)).astype(F32).T
    g_ref[...] = g.T


PEER_TOPK_TILE = 128


def peer_topk(q, keys_bf):
    n = q.shape[0]
    tt = PEER_TOPK_TILE
    spec = pl.BlockSpec((tt, PEER_PAIRS), lambda i: (i, 0))
    shp = jax.ShapeDtypeStruct((n, PEER_PAIRS), F32)
    return pl.pallas_call(
        _peer_topk_kernel,
        out_shape=(shp, shp, shp),
        grid=(n // tt,),
        in_specs=[pl.BlockSpec((tt, PEER_HEADS * PEER_KEY_DIM), lambda i: (i, 0)),
                  pl.BlockSpec((2 * PEER_HEADS, N_KEYS, PEER_HALF), lambda i: (0, 0, 0))],
        out_specs=(spec, spec, spec),
        compiler_params=pltpu.CompilerParams(dimension_semantics=("parallel",),
                                             vmem_limit_bytes=VMEM_LIMIT),
        name="peer_topk",
    )(q, keys_bf)


W_PITCH = 136
PEER_W_TILE = 64


def _peer_w_kernel(a_ref, b_ref, g_ref, o_ref, wf_ref):
    tt = a_ref.shape[0]
    ii = lax.broadcasted_iota(I32, (N_KEYS, PEER_PAIRS), 0).astype(F32)

    def body(t, carry):
        a = a_ref[pl.ds(t, 1), :]
        b = b_ref[pl.ds(t, 1), :]
        g = g_ref[pl.ds(t, 1), :]
        at = jnp.where(ii == a, 1.0, 0.0).astype(BF16)
        bt = jnp.where(ii == b, g, 0.0).astype(BF16)
        w = lax.dot_general(at, bt, NT_DIMS, preferred_element_type=F32)
        wf_ref[pl.ds(pl.multiple_of(t * W_PITCH, 8), N_KEYS), :] = w
        return carry

    lax.fori_loop(0, tt, body, 0, unroll=8)
    for i1 in range(N_KEYS):
        o_ref[:, i1 * N_KEYS:(i1 + 1) * N_KEYS] = wf_ref[pl.ds(i1, tt, stride=W_PITCH), :].astype(BF16)


def peer_gate_matrix(a, b, g):
    n = a.shape[0]
    tt = PEER_W_TILE
    spec = pl.BlockSpec((tt, PEER_PAIRS), lambda i: (i, 0))
    return pl.pallas_call(
        _peer_w_kernel,
        out_shape=jax.ShapeDtypeStruct((n, N_EXPERTS), BF16),
        grid=(n // tt,),
        in_specs=[spec, spec, spec],
        out_specs=pl.BlockSpec((tt, N_EXPERTS), lambda i: (i, 0)),
        scratch_shapes=[pltpu.VMEM((tt * W_PITCH, N_KEYS), F32)],
        compiler_params=pltpu.CompilerParams(dimension_semantics=("parallel",),
                                             vmem_limit_bytes=VMEM_LIMIT),
        name="peer_gate_matrix",
    )(a, b, g)


def _peer_dense_kernel(xn_ref, x_ref, ut_ref, w_ref, v_ref, o_ref):
    @pl.when(pl.program_id(1) == 0)
    def _():
        o_ref[...] = x_ref[...]

    h = jnp.dot(xn_ref[...], ut_ref[...], preferred_element_type=F32)
    act = 0.5 * h * (1.0 + lax.erf(h * np.float32(math.sqrt(0.5))))
    wa = (w_ref[...].astype(F32) * act).astype(BF16)
    o_ref[...] += jnp.dot(wa, v_ref[...], preferred_element_type=F32)


PEER_EXPERT_TILE = 1024


def peer_dense(xn, x, ut, w, v):
    n = x.shape[0]
    tm = _pick_tile(n, (1024, 512, 256, 128))
    te = PEER_EXPERT_TILE
    return pl.pallas_call(
        _peer_dense_kernel,
        out_shape=jax.ShapeDtypeStruct((n, D_MODEL), F32),
        grid=(n // tm, N_EXPERTS // te),
        in_specs=[pl.BlockSpec((tm, D_MODEL), lambda i, j: (i, 0)),
                  pl.BlockSpec((tm, D_MODEL), lambda i, j: (i, 0)),
                  pl.BlockSpec((D_MODEL, te), lambda i, j: (0, j)),
                  pl.BlockSpec((tm, te), lambda i, j: (i, j)),
                  pl.BlockSpec((te, D_MODEL), lambda i, j: (j, 0))],
        out_specs=pl.BlockSpec((tm, D_MODEL), lambda i, j: (i, 0)),
        compiler_params=pltpu.CompilerParams(dimension_semantics=("parallel", "arbitrary"),
                                             vmem_limit_bytes=VMEM_LIMIT),
        name="peer_dense",
    )(xn, x, ut, w, v)


def peer_block(x2, norm2_g, wq_bf, keys_bf, ut_bf, v_bf):
    n = x2.shape[0]
    npad = -(-n // PEER_TOPK_TILE) * PEER_TOPK_TILE
    xp = jnp.pad(x2, ((0, npad - n), (0, 0))) if npad != n else x2
    q, xn = norm_matmul(xp, norm2_g, wq_bf)
    a, b, g = peer_topk(q, keys_bf)
    w = peer_gate_matrix(a, b, g)
    out = peer_dense(xn, xp, ut_bf, w, v_bf)
    return out[:n] if npad != n else out


def _group_sum64(xx, bd):
    hi = xx.astype(BF16)
    r1 = xx - hi.astype(F32)
    mid = r1.astype(BF16)
    lo = (r1 - mid.astype(F32)).astype(BF16)
    dot = lambda a: jnp.dot(a, bd, preferred_element_type=F32)
    return dot(hi) + dot(mid) + dot(lo)


def _rope128(y, c, s, lo_half):
    partner = jnp.where(lo_half, pltpu.roll(y, LANES - 32, axis=1), pltpu.roll(y, 32, axis=1))
    return y * c + partner * s


def _attn_prep_kernel(q_ref, kv_ref, iq_ref, sm_ref, cos_ref, sin_ref, qg_ref, kg_ref,
                      qo_ref, ko_ref, vo_ref, iqo_ref, iko_ref):
    tm = q_ref.shape[0]
    lane = lax.broadcasted_iota(I32, (tm, LANES), 1)
    lo_half = (lane & (HEAD_DIM - 1)) < HEAD_DIM // 2
    grp_r = lax.broadcasted_iota(I32, (LANES, LANES), 0) >> 6
    grp_c = lax.broadcasted_iota(I32, (LANES, LANES), 1) >> 6
    bd = jnp.where(grp_r == grp_c, 1.0, 0.0).astype(BF16)
    c = cos_ref[...]
    s = sin_ref[...]

    def normed(x, g):
        ms = _group_sum64(x * x, bd) * np.float32(1.0 / HEAD_DIM)
        return x * lax.rsqrt(ms + EPS) * g

    for j in range(ATT_WIDTH // LANES):
        sl = slice(j * LANES, (j + 1) * LANES)
        qo_ref[:, sl] = _rope128(normed(q_ref[:, sl], qg_ref[...]), c, s, lo_half)
    for j in range(KV_WIDTH // LANES):
        sl = slice(j * LANES, (j + 1) * LANES)
        ko_ref[:, sl] = _rope128(normed(kv_ref[:, sl], kg_ref[...]), c, s, lo_half)
    vo_ref[...] = kv_ref[:, KV_WIDTH:2 * KV_WIDTH]
    for j in range(IDX_HEADS * IDX_DIM // LANES):
        sl = slice(j * LANES, (j + 1) * LANES)
        iqo_ref[:, sl] = _rope128(iq_ref[:, sl], c, s, lo_half)
    iko_ref[...] = _rope128(sm_ref[...], c, s, lo_half)[:, SMALL_IK:SMALL_IK + IDX_DIM]


def attn_prep(proj, cos_t, sin_t, q_norm_g, k_norm_g):
    n = proj.shape[0]
    tm = _pick_tile(n, (256, 128))
    nt = cos_t.shape[0] // tm
    row = lambda w, off: pl.BlockSpec((tm, w), lambda i: (i, off // w))
    tab = pl.BlockSpec((tm, LANES), lambda i: (i % nt, 0))
    gain = pl.BlockSpec((1, LANES), lambda i: (0, 0))
    out = lambda w: pl.BlockSpec((tm, w), lambda i: (i, 0))
    shp = lambda w: jax.ShapeDtypeStruct((n, w), F32)
    tile2 = lambda g: jnp.tile(g.reshape(1, HEAD_DIM), (1, LANES // HEAD_DIM))
    return pl.pallas_call(
        _attn_prep_kernel,
        out_shape=(shp(ATT_WIDTH), shp(KV_WIDTH), shp(KV_WIDTH), shp(IDX_HEADS * IDX_DIM), shp(IDX_DIM)),
        grid=(n // tm,),
        in_specs=[row(ATT_WIDTH, OFF_Q), row(2 * KV_WIDTH, OFF_KV), row(IDX_HEADS * IDX_DIM, OFF_IQ),
                  row(LANES, OFF_SMALL), tab, tab, gain, gain],
        out_specs=(out(ATT_WIDTH), out(KV_WIDTH), out(KV_WIDTH), out(IDX_HEADS * IDX_DIM), out(IDX_DIM)),
        compiler_params=pltpu.CompilerParams(dimension_semantics=("parallel",),
                                             vmem_limit_bytes=VMEM_LIMIT),
        name="attn_prep",
    )(proj, proj, proj, proj, cos_t, sin_t, tile2(q_norm_g), tile2(k_norm_g))


def rope_tables(pos):
    inv = ROPE_THETA ** (-jnp.arange(0, HEAD_DIM, 2, dtype=F32) / HEAD_DIM)
    ang = pos.astype(F32)[:, None] * inv[None, :]
    cos, sin = jnp.cos(ang), jnp.sin(ang)
    reps = LANES // HEAD_DIM
    return (jnp.tile(jnp.concatenate([cos, cos], axis=-1), (1, reps)),
            jnp.tile(jnp.concatenate([-sin, sin], axis=-1), (1, reps)))


SSD_HEAD_LANE = SMALL_DT
XS_W = D_INNER
BC_W = 2 * SSM_GROUPS * D_STATE
GROUP_W = SSM_HPG * SSM_HEAD_DIM
CONV_COL_CHUNK = 512


def _split3(x):
    hi = x.astype(BF16)
    r1 = x - hi.astype(F32)
    mid = r1.astype(BF16)
    return hi, mid, (r1 - mid.astype(F32)).astype(BF16)


def _ssd_kernel(z_ref, xs_ref, bc_ref, sm_ref, cw_ref, cb_ref, dtb_ref, alog_ref, dskip_ref, ng_ref,
                exp_ref, y_ref, h_ref, cn_ref, xp_ref, u_ref, ht_ref):
    c = pl.program_id(1)
    q = SSD_CHUNK

    @pl.when(c == 0)
    def _():
        xp_ref[0:8, :] = jnp.zeros((8, CONV_DIM), F32)
        ht_ref[...] = jnp.zeros(ht_ref.shape, F32)

    xp_ref[8:8 + q, 0:XS_W] = xs_ref[...]
    xp_ref[8:8 + q, XS_W:CONV_DIM] = bc_ref[...]
    for cc in range(CONV_DIM // CONV_COL_CHUNK):
        sl = slice(cc * CONV_COL_CHUNK, (cc + 1) * CONV_COL_CHUNK)
        conv = cb_ref[:, sl]
        for j in range(CONV_W):
            conv = conv + xp_ref[8 - (CONV_W - 1) + j:8 - (CONV_W - 1) + j + q, sl] * cw_ref[j:j + 1, sl]
        u_ref[:, sl] = conv * jax.nn.sigmoid(conv)

    @pl.when(c == pl.num_programs(1) - 1)
    def _():
        cn_ref[...] = xp_ref[8 + q - (CONV_W - 1):8 + q, :]

    xp_ref[0:8, :] = xp_ref[q:q + 8, :]

    lane = lax.broadcasted_iota(I32, (q, LANES), 1)
    head_lane = (lane >= SSD_HEAD_LANE) & (lane < SSD_HEAD_LANE + SSM_HEADS)
    pre = sm_ref[...] + dtb_ref[...]
    dt = jnp.where(head_lane, jnp.maximum(pre, 0.0) + jnp.log1p(jnp.exp(-jnp.abs(pre))), 0.0)
    ad = dt * (-jnp.exp(alog_ref[...]))
    row = lax.broadcasted_iota(I32, (q, q), 0)
    col = lax.broadcasted_iota(I32, (q, q), 1)
    tril = row >= col
    tri = jnp.where(tril, 1.0, 0.0).astype(BF16)
    acs = sum(jnp.dot(tri, part, preferred_element_type=F32) for part in _split3(ad))
    acs_last = acs[q - 1:q, :]
    expand = exp_ref[...]
    e1 = jnp.dot(jnp.exp(acs).astype(BF16), expand, preferred_element_type=F32)
    e2 = jnp.dot((jnp.exp(acs_last - acs) * dt).astype(BF16), expand, preferred_element_type=F32)
    dec8 = jnp.broadcast_to(jnp.exp(acs_last), (8, LANES))
    dec = sum(jnp.dot(part, expand, preferred_element_type=F32) for part in _split3(dec8))[0:1, :]
    acs_t = acs.T
    dt_t = dt.T
    lane_lo = lane < SSM_HEAD_DIM

    for g in range(SSM_GROUPS):
        gs = slice(g * GROUP_W, (g + 1) * GROUP_W)
        bg = u_ref[:, XS_W + g * D_STATE:XS_W + (g + 1) * D_STATE]
        cg = u_ref[:, XS_W + BC_W // 2 + g * D_STATE:XS_W + BC_W // 2 + (g + 1) * D_STATE].astype(BF16)
        cb = lax.dot_general(cg, bg.astype(BF16), NT_DIMS, preferred_element_type=F32)
        xs_g = u_ref[:, gs]
        y_diag = []
        for jp in range(SSM_HPG // 2):
            ms = []
            for r in (g * SSM_HPG + 2 * jp, g * SSM_HPG + 2 * jp + 1):
                hl = SSD_HEAD_LANE + r
                seg = acs[:, hl:hl + 1] - acs_t[hl:hl + 1, :]
                lmat = jnp.where(tril, jnp.exp(seg), 0.0)
                ms.append((cb * lmat * dt_t[hl:hl + 1, :]).astype(BF16))
            blk = xs_g[:, jp * LANES:(jp + 1) * LANES]
            rhs = jnp.concatenate([jnp.where(lane_lo, blk, 0.0), jnp.where(lane_lo, 0.0, blk)], axis=0)
            y_diag.append(jnp.dot(jnp.concatenate(ms, axis=1), rhs.astype(BF16), preferred_element_type=F32))
        h_old = ht_ref[:, gs]
        y_off = jnp.dot(cg, h_old.astype(BF16), preferred_element_type=F32) * e1[:, gs]
        y = jnp.concatenate(y_diag, axis=1) + y_off + dskip_ref[:, gs] * xs_g
        zz = z_ref[:, gs]
        y = y * (zz * jax.nn.sigmoid(zz))
        ms_y = jnp.sum(y * y, axis=-1, keepdims=True) * np.float32(1.0 / GROUP_W)
        y_ref[:, gs] = (y * lax.rsqrt(ms_y + EPS) * ng_ref[:, gs]).astype(BF16)
        upd = jnp.dot(bg.T.astype(BF16), (xs_g * e2[:, gs]).astype(BF16), preferred_element_type=F32)
        ht_ref[:, gs] = h_old * dec[:, gs] + upd

    @pl.when(c == pl.num_programs(1) - 1)
    def _():
        for j in range(D_INNER // LANES):
            h_ref[j * LANES:(j + 1) * LANES, :] = ht_ref[:, j * LANES:(j + 1) * LANES].T


def ssd_prompt(proj, conv_w, conv_b, dt_bias, a_log, d_skip, norm_g, bsz, seq):
    nc = seq // SSD_CHUNK
    n = bsz * seq
    rmap = lambda blk: (lambda b, c: (b * nc + c, blk))
    const = lambda a: pl.BlockSpec(a.shape, lambda b, c: (0, 0))
    head_row = lambda v: jnp.zeros((1, LANES), F32).at[0, SSD_HEAD_LANE:SSD_HEAD_LANE + SSM_HEADS].set(v)
    expand = (jnp.arange(LANES)[:, None] == SSD_HEAD_LANE + jnp.arange(D_INNER)[None, :] // SSM_HEAD_DIM).astype(BF16)
    consts = (conv_w, conv_b.reshape(1, CONV_DIM), head_row(dt_bias), head_row(a_log),
              jnp.repeat(d_skip, SSM_HEAD_DIM).reshape(1, D_INNER), norm_g.reshape(1, D_INNER), expand)
    return pl.pallas_call(
        _ssd_kernel,
        out_shape=(jax.ShapeDtypeStruct((n, D_INNER), BF16),
                   jax.ShapeDtypeStruct((bsz, D_INNER, D_STATE), F32),
                   jax.ShapeDtypeStruct((bsz, CONV_W - 1, CONV_DIM), F32)),
        grid=(bsz, nc),
        in_specs=[pl.BlockSpec((SSD_CHUNK, D_INNER), rmap(OFF_Z // D_INNER)),
                  pl.BlockSpec((SSD_CHUNK, XS_W), rmap(OFF_XBC // XS_W)),
                  pl.BlockSpec((SSD_CHUNK, BC_W), rmap((OFF_XBC + XS_W) // BC_W)),
                  pl.BlockSpec((SSD_CHUNK, LANES), rmap(OFF_SMALL // LANES))] + [const(a) for a in consts],
        out_specs=(pl.BlockSpec((SSD_CHUNK, D_INNER), rmap(0)),
                   pl.BlockSpec((None, D_INNER, D_STATE), lambda b, c: (b, 0, 0)),
                   pl.BlockSpec((None, CONV_W - 1, CONV_DIM), lambda b, c: (b, 0, 0))),
        scratch_shapes=[pltpu.VMEM((SSD_CHUNK + 8, CONV_DIM), F32),
                        pltpu.VMEM((SSD_CHUNK, CONV_DIM), F32),
                        pltpu.VMEM((D_STATE, D_INNER), F32)],
        compiler_params=pltpu.CompilerParams(dimension_semantics=("parallel", "arbitrary"),
                                             vmem_limit_bytes=VMEM_LIMIT),
        name="ssd_prompt",
    )(proj, proj, proj, proj, *consts)


def _mix_kernel(ya_ref, yb_ref, ga_ref, gb_ref, x_ref, wa_ref, wb_ref, wo_ref, o_ref):
    ma = jnp.dot(ya_ref[...].astype(BF16), wa_ref[...], preferred_element_type=F32)
    mb = jnp.dot(yb_ref[...].astype(BF16), wb_ref[...], preferred_element_type=F32)
    mixed = jax.nn.sigmoid(ga_ref[...]) * ma + jax.nn.sigmoid(gb_ref[...]) * mb
    o_ref[...] = x_ref[...] + jnp.dot(mixed.astype(BF16), wo_ref[...], preferred_element_type=F32)


def branch_mix(ya, yb, proj, x, wa, wb, wo):
    n = x.shape[0]
    tm = _pick_tile(n, (256, 128))
    rows = lambda w, blk=0: pl.BlockSpec((tm, w), lambda i: (i, blk))
    whole = lambda a: pl.BlockSpec(a.shape, lambda i: (0, 0))
    return pl.pallas_call(
        _mix_kernel,
        out_shape=jax.ShapeDtypeStruct((n, D_MODEL), F32),
        grid=(n // tm,),
        in_specs=[rows(D_INNER), rows(ATT_WIDTH), rows(D_MODEL, OFF_GA // D_MODEL),
                  rows(D_MODEL, OFF_GB // D_MODEL), rows(D_MODEL), whole(wa), whole(wb), whole(wo)],
        out_specs=rows(D_MODEL),
        compiler_params=pltpu.CompilerParams(dimension_semantics=("parallel",),
                                             vmem_limit_bytes=VMEM_LIMIT),
        name="branch_mix",
    )(ya, yb, proj, proj, x, wa, wb, wo)


def _dsa_prompt_body(s_len, qb, iq_ref, sm_ref, ik_ref, q_ref, k_ref, v_ref, o_ref, jc_ref):
    ik = ik_ref[0:s_len, :].astype(BF16)
    iw = sm_ref[:, SMALL_IW:SMALL_IW + IDX_HEADS] * np.float32(IDX_SCALE)
    score = jnp.zeros((Q_BLOCK, s_len), F32)
    for h in range(IDX_HEADS):
        iqh = iq_ref[:, h * IDX_DIM:(h + 1) * IDX_DIM].astype(BF16)
        d = lax.dot_general(iqh, ik, NT_DIMS, preferred_element_type=F32)
        score = score + jnp.maximum(d, 0.0) * iw[:, h:h + 1]

    tpos = qb * Q_BLOCK + lax.broadcasted_iota(I32, (Q_BLOCK, 1), 0)
    spos = lax.broadcasted_iota(I32, (1, s_len), 1)
    causal = spos <= tpos
    bits = lax.bitcast_convert_type(score, I32)
    key = bits ^ ((bits >> 31) & np.int32(0x7FFFFFFF))
    key = jnp.where(score == 0.0, 0, key)
    key = jnp.where(causal, key, INT_MIN)

    def count(mask):
        return jnp.sum(mask.astype(I32), axis=-1, keepdims=True)

    def bisect_value(i, tu):
        cand_u = tu | jnp.left_shift(np.int32(1), 31 - i)
        return jnp.where(count(key >= (cand_u ^ INT_MIN)) >= TOPK_MAX, cand_u, tu)

    thr = lax.fori_loop(0, 32, bisect_value, jnp.zeros((Q_BLOCK, 1), I32)) ^ INT_MIN
    above = key > thr
    tied = key == thr
    need = TOPK_MAX - count(above)
    excess = (count(tied) > need) & (thr != INT_MIN)

    jc_ref[...] = jnp.full((Q_BLOCK, LANES), s_len, I32)

    @pl.when(jnp.max(excess.astype(I32)) > 0)
    def _():
        def bisect_pos(i, j):
            cj = j | jnp.left_shift(np.int32(1), 10 - i)
            return jnp.where(count(tied & (spos < cj)) < need, cj, j)

        jc = lax.fori_loop(0, 11, bisect_pos, jnp.zeros((Q_BLOCK, 1), I32))
        jc_ref[...] = jnp.broadcast_to(jc, (Q_BLOCK, LANES))

    jc = jc_ref[:, 0:1]
    sel = causal & (above | (tied & (spos <= jc)))
    bias = jnp.where(sel, 0.0, -jnp.inf)

    for g in range(KV_HEADS):
        qs = jnp.concatenate(
            [q_ref[:, (ATT_REP * g + r) * HEAD_DIM:(ATT_REP * g + r + 1) * HEAD_DIM] for r in range(ATT_REP)],
            axis=0).astype(BF16)
        kg = k_ref[0:s_len, g * HEAD_DIM:(g + 1) * HEAD_DIM].astype(BF16)
        vg = v_ref[0:s_len, g * HEAD_DIM:(g + 1) * HEAD_DIM].astype(BF16)
        s = lax.dot_general(qs, kg, NT_DIMS, preferred_element_type=F32) * np.float32(HEAD_DIM ** -0.5)
        s = (s.reshape(ATT_REP, Q_BLOCK, s_len) + bias[None]).reshape(ATT_REP * Q_BLOCK, s_len)
        m = jnp.max(s, axis=-1, keepdims=True)
        p = jnp.exp(s - m)
        l = jnp.sum(p, axis=-1, keepdims=True)
        o = jnp.dot(p.astype(BF16), vg, preferred_element_type=F32) / l
        for r in range(ATT_REP):
            hh = ATT_REP * g + r
            o_ref[:, hh * HEAD_DIM:(hh + 1) * HEAD_DIM] = o[r * Q_BLOCK:(r + 1) * Q_BLOCK]


DSA_KEY_BUCKET = 512


def _dsa_prompt_kernel(*refs):
    qb = pl.program_id(1)
    s_full = refs[2].shape[0]
    per = DSA_KEY_BUCKET // Q_BLOCK
    for j in range(s_full // DSA_KEY_BUCKET):
        pl.when((qb >= per * j) & (qb < per * (j + 1)))(
            functools.partial(_dsa_prompt_body, DSA_KEY_BUCKET * (j + 1), qb, *refs))


def dsa_prompt(q, k, v, iq, ik, proj, bsz, seq):
    nq = seq // Q_BLOCK
    assert seq == 2048 and TOPK_MAX <= seq // 4
    qmap = lambda b, i: (b * nq + i, 0)
    bmap = lambda b, i: (b, 0)
    return pl.pallas_call(
        _dsa_prompt_kernel,
        out_shape=jax.ShapeDtypeStruct((bsz * seq, ATT_WIDTH), F32),
        grid=(bsz, nq),
        in_specs=[pl.BlockSpec((Q_BLOCK, IDX_HEADS * IDX_DIM), qmap),
                  pl.BlockSpec((Q_BLOCK, LANES), lambda b, i: (b * nq + i, OFF_SMALL // LANES)),
                  pl.BlockSpec((seq, IDX_DIM), bmap),
                  pl.BlockSpec((Q_BLOCK, ATT_WIDTH), qmap),
                  pl.BlockSpec((seq, KV_WIDTH), bmap),
                  pl.BlockSpec((seq, KV_WIDTH), bmap)],
        out_specs=pl.BlockSpec((Q_BLOCK, ATT_WIDTH), qmap),
        scratch_shapes=[pltpu.VMEM((Q_BLOCK, LANES), I32)],
        compiler_params=pltpu.CompilerParams(dimension_semantics=("parallel", "arbitrary"),
                                             vmem_limit_bytes=VMEM_LIMIT),
        name="dsa_prompt",
    )(iq, proj, ik, q, k, v)


def _rms(x, g):
    xf = x.astype(F32)
    y = xf * lax.rsqrt(jnp.mean(xf * xf, axis=-1, keepdims=True) + EPS)
    return y * g.astype(F32)


def _rope(x, pos):
    d = x.shape[-1]
    inv = ROPE_THETA ** (-jnp.arange(0, d, 2, dtype=F32) / d)
    ang = pos.astype(F32)[:, None] * inv[None, :]
    ang = ang.reshape((ang.shape[0],) + (1,) * (x.ndim - 3) + (d // 2,))
    cos, sin = jnp.cos(ang), jnp.sin(ang)
    x1, x2 = x[..., :d // 2], x[..., d // 2:]
    return jnp.concatenate([x1 * cos - x2 * sin, x2 * cos + x1 * sin], axis=-1)


def _gather_rows(rows, idx):
    return jax.vmap(lambda r, i: r[i])(rows, idx)


def _ssd_chunked(xs, dt, a, bm, cm, h0):
    bsz, seq = xs.shape[:2]
    nc = seq // SSD_CHUNK

    def to_chunks(t):
        return jnp.moveaxis(t.reshape((bsz, nc, SSD_CHUNK) + t.shape[2:]), 1, 0)

    xd = xs * dt[..., None]
    ad = dt * a
    causal = jnp.tril(jnp.ones((SSD_CHUNK, SSD_CHUNK), dtype=bool))

    def step(h, inp):
        xd_c, ad_c, b_c, c_c = inp
        acs = jnp.cumsum(ad_c, axis=1)
        seg = acs[:, :, None] - acs[:, None, :]
        lmat = jnp.exp(jnp.where(causal[None, :, :, None, None], seg, -jnp.inf))
        cb = jnp.einsum('blgn,bsgn->blsg', c_c, b_c)
        y_diag = jnp.einsum('blsg,blsgr,bsgrp->blgrp', cb, lmat, xd_c)
        y_off = jnp.einsum('blgn,bgrpn,blgr->blgrp', c_c, h, jnp.exp(acs))
        decay_end = jnp.exp(acs[:, -1:] - acs)
        h_new = h * jnp.exp(acs[:, -1])[..., None, None] + jnp.einsum('bsgn,bsgr,bsgrp->bgrpn', b_c, decay_end, xd_c)
        return h_new, y_diag + y_off

    h, ys = lax.scan(step, h0, (to_chunks(xd), to_chunks(ad), to_chunks(bm), to_chunks(cm)))
    y = jnp.moveaxis(ys, 0, 1).reshape(xs.shape)
    return y, h


def _ssd_recurrent(xs, dt, a, bm, cm, h0):
    def step(h, inp):
        x_t, dt_t, b_t, c_t = inp
        h = h * jnp.exp(dt_t * a)[..., None, None] + jnp.einsum('bgrp,bgn->bgrpn', x_t * dt_t[..., None], b_t)
        return h, jnp.einsum('bgn,bgrpn->bgrp', c_t, h)

    tf = lambda t: jnp.moveaxis(t, 1, 0)
    h, ys = lax.scan(step, h0, (tf(xs), tf(dt), tf(bm), tf(cm)))
    return jnp.moveaxis(ys, 0, 1), h


def _mamba2(z, xbc, dt_raw, conv_prefix, h0, conv_w, conv_b, dt_bias, a_log, d_skip, norm_g, chunked):
    bsz, t_len, _ = z.shape
    xp = jnp.concatenate([conv_prefix, xbc], axis=1)
    conv = conv_b + sum(xp[:, j:j + t_len] * conv_w[j] for j in range(CONV_W))
    conv_new = xp[:, t_len:]
    xbc = jax.nn.silu(conv)
    xs, bm, cm = jnp.split(xbc, [D_INNER, D_INNER + SSM_GROUPS * D_STATE], axis=-1)
    xs = xs.reshape(bsz, t_len, SSM_GROUPS, SSM_HPG, SSM_HEAD_DIM)
    bm = bm.reshape(bsz, t_len, SSM_GROUPS, D_STATE)
    cm = cm.reshape(bsz, t_len, SSM_GROUPS, D_STATE)
    dt = jax.nn.softplus(dt_raw + dt_bias).reshape(bsz, t_len, SSM_GROUPS, SSM_HPG)
    a = -jnp.exp(a_log).reshape(SSM_GROUPS, SSM_HPG)
    scan_fn = _ssd_chunked if chunked else _ssd_recurrent
    y, h = scan_fn(xs, dt, a, bm, cm, h0)
    y = y + d_skip.reshape(SSM_GROUPS, SSM_HPG, 1) * xs
    y = y.reshape(bsz, t_len, D_INNER) * jax.nn.silu(z)
    y = _rms(y.reshape(bsz, t_len, SSM_GROUPS, D_INNER // SSM_GROUPS), norm_g.reshape(SSM_GROUPS, -1))
    return y.reshape(bsz, t_len, D_INNER), h, conv_new


def _indexer_scores(iq, ik, iw):
    dots = jax.nn.relu(jnp.einsum('bthd,bsd->bths', iq, ik))
    return jnp.einsum('bths,bth->bts', dots, iw * IDX_SCALE)


def _sparse_attend(q, ks, vs, valid):
    bsz, t_len = q.shape[:2]
    qg = q.reshape(bsz, t_len, KV_HEADS, ATT_REP, HEAD_DIM)
    s = jnp.einsum('btgrd,btkgd->btgrk', qg, ks) * (HEAD_DIM ** -0.5)
    s = jnp.where(valid[:, :, None, None, :], s, -jnp.inf)
    p = jax.nn.softmax(s, axis=-1)
    o = jnp.einsum('btgrk,btkgd->btgrd', p, vs)
    return o.reshape(bsz, t_len, ATT_WIDTH)


def _dsa_sample(q, k, v, iq, ik, iw, ck, cv, cik, page_table):
    bsz, t_len = q.shape[:2]
    past = page_table.shape[1] * PAGE_SIZE
    n_keys = past + t_len
    topk = min(TOPK_MAX, n_keys // 4)
    ik_past = cik[page_table].reshape(bsz, past, IDX_DIM)
    score = _indexer_scores(iq, jnp.concatenate([ik_past, ik], axis=1), iw)
    qpos = past + jnp.arange(t_len)
    allowed = jnp.arange(n_keys)[None, :] <= qpos[:, None]
    score = jnp.where(allowed[None], score, -jnp.inf)
    _, idx = lax.top_k(score, topk)
    valid = idx <= qpos[None, :, None]
    from_new = idx >= past
    p_idx = jnp.minimum(idx, past - 1)
    phys = jax.vmap(lambda pt, i: pt[i // PAGE_SIZE])(page_table, p_idx)
    off = p_idx % PAGE_SIZE
    n_idx = jnp.clip(idx - past, 0, t_len - 1)

    def select(cache, new):
        return jnp.where(from_new[..., None, None], _gather_rows(new, n_idx), cache[phys, off])

    return _sparse_attend(q, select(ck, k), select(cv, v), valid)


def _layer(x, pos, lw, conv_prefix, h0, paged):
    (norm1_g, w_in_bf, conv_w, conv_b, dt_bias, a_log, d_skip, ssm_norm_g, q_norm_g, k_norm_g,
     w_a_bf, w_b_bf, w_out_bf, norm2_g, wq_bf, keys_bf, ut_bf, v_bf) = lw
    bsz, t_len, _ = x.shape
    n = bsz * t_len
    x2 = x.reshape(n, D_MODEL)
    proj, _ = norm_matmul(x2, norm1_g, w_in_bf)
    cos_t, sin_t = rope_tables(pos)
    q, k, v, iq, ik = attn_prep(proj, cos_t, sin_t, q_norm_g, k_norm_g)
    seg = lambda off, w: proj[:, off:off + w].reshape(bsz, t_len, w)
    if paged is None:
        y_a, h_new, conv_new = ssd_prompt(proj, conv_w, conv_b, dt_bias, a_log, d_skip, ssm_norm_g, bsz, t_len)
    else:
        y_a, h_new, conv_new = _mamba2(seg(OFF_Z, D_INNER), seg(OFF_XBC, CONV_DIM),
                                       seg(OFF_SMALL + SMALL_DT, SSM_HEADS), conv_prefix, h0, conv_w, conv_b,
                                       dt_bias, a_log, d_skip, ssm_norm_g, False)
    k4 = k.reshape(bsz, t_len, KV_HEADS, HEAD_DIM)
    v4 = v.reshape(bsz, t_len, KV_HEADS, HEAD_DIM)
    ik3 = ik.reshape(bsz, t_len, IDX_DIM)
    if paged is None:
        y_b = dsa_prompt(q, k, v, iq, ik, proj, bsz, t_len)
    else:
        y_b = _dsa_sample(q.reshape(bsz, t_len, ATT_HEADS, HEAD_DIM), k4, v4,
                          iq.reshape(bsz, t_len, IDX_HEADS, IDX_DIM), ik3,
                          seg(OFF_SMALL + SMALL_IW, IDX_HEADS), *paged).reshape(n, ATT_WIDTH)
    x2 = branch_mix(y_a.reshape(n, D_INNER), y_b, proj, x2, w_a_bf, w_b_bf, w_out_bf)
    x2 = peer_block(x2, norm2_g, wq_bf, keys_bf, ut_bf, v_bf)
    h_out = h_new.reshape(bsz, SSM_HEADS, SSM_HEAD_DIM, D_STATE)
    return x2.reshape(bsz, t_len, D_MODEL), k4, v4, ik3, h_out, conv_new


def kernel(x_prompt, x_sample, cache_k, cache_v, cache_idx_k, state_ssm, state_conv, page_table,
           norm1_g, w_in, conv_w, conv_b, dt_bias, a_log, d_skip, ssm_norm_g, q_norm_g, k_norm_g,
           w_branch_a, w_branch_b, w_out, norm2_g, peer_wq, peer_keys, peer_u, peer_v):
    bp, seq = x_prompt.shape[:2]
    bs, t_new = x_sample.shape[:2]
    past = page_table.shape[1] * PAGE_SIZE
    pos_p = jnp.arange(seq)
    pos_s = jnp.tile(past + jnp.arange(t_new), bs)
    yp, ys = x_prompt, x_sample
    outs_p, outs_s = [], []
    sizes = dict(zip(("z", "xbc", "dt", "q", "k", "v", "iq", "ik", "iw", "ga", "gb"), IN_SIZES))
    starts = dict(zip(sizes, np.cumsum((0,) + IN_SIZES[:-1])))
    order = ("z", "xbc", "q", "k", "v", "iq", "ga", "gb", "ik", "dt", "iw")
    for l in range(DEPTH):
        w_bf = w_in[l].astype(BF16)
        cols = [w_bf[:, int(starts[s]):int(starts[s]) + sizes[s]] for s in order]
        used = sum(sizes.values())
        w_in_pad = jnp.concatenate(cols + [jnp.zeros((D_MODEL, IN_WIDTH_PAD - used), BF16)], axis=1)
        lw = (norm1_g[l], w_in_pad, conv_w[l], conv_b[l], dt_bias[l], a_log[l], d_skip[l],
              ssm_norm_g[l], q_norm_g[l], k_norm_g[l], w_branch_a[l].astype(BF16),
              w_branch_b[l].astype(BF16), w_out[l].astype(BF16), norm2_g[l],
              peer_wq[l].astype(BF16),
              peer_keys[l].astype(BF16).reshape(2 * PEER_HEADS, N_KEYS, PEER_HALF),
              peer_u[l].T.astype(BF16), peer_v[l].astype(BF16))
        conv0 = jnp.zeros((bp, CONV_W - 1, CONV_DIM), F32)
        h0 = jnp.zeros((bp, SSM_GROUPS, SSM_HPG, SSM_HEAD_DIM, D_STATE), F32)
        yp, *rest = _layer(yp, pos_p, lw, conv0, h0, None)
        outs_p.append(rest)
        h_s = state_ssm[l].reshape(bs, SSM_GROUPS, SSM_HPG, SSM_HEAD_DIM, D_STATE)
        ys, *rest = _layer(ys, pos_s, lw, state_conv[l], h_s,
                           (cache_k[l], cache_v[l], cache_idx_k[l], page_table))
        outs_s.append(rest)
    stack = lambda outs, i: jnp.stack([o[i] for o in outs])
    return (yp, ys,
            stack(outs_p, 0), stack(outs_p, 1), stack(outs_p, 2), stack(outs_p, 3), stack(outs_p, 4),
            stack(outs_s, 0), stack(outs_s, 1), stack(outs_s, 2), stack(outs_s, 3), stack(outs_s, 4))
```

```python
import functools
import math

import jax
import jax.numpy as jnp
import numpy as np
from jax import lax
from jax.experimental import pallas as pl
from jax.experimental.pallas import tpu as pltpu

D_MODEL = 1024
DEPTH = 2
PAGE_SIZE = 128
D_INNER = 2048
SSM_HEAD_DIM = 64
SSM_HEADS = 32
SSM_GROUPS = 4
SSM_HPG = 8
D_STATE = 128
CONV_W = 4
CONV_DIM = D_INNER + 2 * SSM_GROUPS * D_STATE
SSD_CHUNK = 128
ATT_HEADS = 16
KV_HEADS = 4
HEAD_DIM = 64
ATT_REP = 4
ATT_WIDTH = 1024
KV_WIDTH = 256
IDX_HEADS = 8
IDX_DIM = 64
IDX_SCALE = (IDX_HEADS * IDX_DIM) ** -0.5
TOPK_MAX = 256
Q_BLOCK = 128
ROPE_THETA = 10000.0
PEER_HEADS = 8
N_KEYS = 128
N_EXPERTS = N_KEYS * N_KEYS
PEER_KEY_DIM = 256
PEER_HALF = 128
PEER_TOPK = 16
PEER_PAIRS = PEER_HEADS * PEER_TOPK
EPS = 1e-6
IN_SIZES = (D_INNER, CONV_DIM, SSM_HEADS, ATT_WIDTH, KV_WIDTH, KV_WIDTH,
            IDX_HEADS * IDX_DIM, IDX_DIM, IDX_HEADS, D_MODEL, D_MODEL)
IN_WIDTH = sum(IN_SIZES)
OFF_Z, OFF_XBC, OFF_Q, OFF_KV, OFF_IQ, OFF_GA, OFF_GB, OFF_SMALL = 0, 2048, 5120, 6144, 6656, 7168, 8192, 9216
SMALL_IK, SMALL_DT, SMALL_IW = 0, 64, 96
IN_WIDTH_PAD = 9728

F32 = jnp.float32
BF16 = jnp.bfloat16
I32 = jnp.int32
INT_MIN = np.int32(-2 ** 31)
VMEM_LIMIT = 56 * 1024 * 1024
LANES = 128
BF16_ROWS = 16
NT_DIMS = (((1,), (1,)), ((), ()))


def _pick_tile(n, candidates):
    for c in candidates:
        if n % c == 0:
            return c
    return n


def _mm_norm_kernel(x_ref, g_ref, w_ref, o_ref, xn_ref):
    @pl.when(pl.program_id(1) == 0)
    def _():
        x = x_ref[...]
        ms = jnp.mean(x * x, axis=-1, keepdims=True)
        xn_ref[...] = (x * lax.rsqrt(ms + EPS) * g_ref[...]).astype(BF16)

    o_ref[...] = jnp.dot(xn_ref[...], w_ref[...], preferred_element_type=F32)


def _mm_kernel(x_ref, w_ref, o_ref):
    o_ref[...] = jnp.dot(x_ref[...].astype(BF16), w_ref[...], preferred_element_type=F32)


def matmul(x, w):
    m, k = x.shape
    n = w.shape[1]
    tm = _pick_tile(m, (1024, 512, 256, 128))
    tn = _pick_tile(n, (512, 256, 128))
    return pl.pallas_call(
        _mm_kernel,
        out_shape=jax.ShapeDtypeStruct((m, n), F32),
        grid=(m // tm, n // tn),
        in_specs=[pl.BlockSpec((tm, k), lambda i, j: (i, 0)),
                  pl.BlockSpec((k, tn), lambda i, j: (0, j))],
        out_specs=pl.BlockSpec((tm, tn), lambda i, j: (i, j)),
        compiler_params=pltpu.CompilerParams(dimension_semantics=("parallel", "arbitrary"),
                                             vmem_limit_bytes=VMEM_LIMIT),
        name="matmul",
    )(x, w)


def norm_matmul(x, gain, w):
    m, k = x.shape
    n = w.shape[1]
    tm = _pick_tile(m, (1024, 512, 256, 128))
    tn = _pick_tile(n, (512, 256, 128))
    return pl.pallas_call(
        _mm_norm_kernel,
        out_shape=(jax.ShapeDtypeStruct((m, n), F32), jax.ShapeDtypeStruct((m, k), BF16)),
        grid=(m // tm, n // tn),
        in_specs=[pl.BlockSpec((tm, k), lambda i, j: (i, 0)),
                  pl.BlockSpec((1, k), lambda i, j: (0, 0)),
                  pl.BlockSpec((k, tn), lambda i, j: (0, j))],
        out_specs=(pl.BlockSpec((tm, tn), lambda i, j: (i, j)),
                   pl.BlockSpec((tm, k), lambda i, j: (i, 0))),
        compiler_params=pltpu.CompilerParams(dimension_semantics=("parallel", "arbitrary"),
                                             vmem_limit_bytes=VMEM_LIMIT),
        name="norm_matmul",
    )(x, gain.reshape(1, k), w)


def _top16_rows(s):
    r = s.shape[0]
    rows = lax.broadcasted_iota(I32, s.shape, 0)
    vals, idxs = [], []
    for _ in range(PEER_TOPK):
        m = jnp.max(s, axis=0, keepdims=True)
        i = jnp.min(jnp.where(s == m, rows, r), axis=0, keepdims=True)
        vals.append(m)
        idxs.append(i)
        s = jnp.where(rows == i, -jnp.inf, s)
    return jnp.concatenate(vals, axis=0), jnp.concatenate(idxs, axis=0)


def _peer_pairs(sv0, si0, sv1, si1):
    t = sv0.shape[1]
    row8 = lax.broadcasted_iota(I32, (8, t), 0)
    row16 = lax.broadcasted_iota(I32, (16, t), 0)
    vals = [sv0[0:1] + sv1]
    eids = [si0[0:1] * N_KEYS + si1]
    flat = [row16]
    for a in range(1, 8):
        nb = PEER_TOPK // (a + 1)
        v = sv0[a:a + 1] + sv1[0:8]
        vals.append(v if nb >= 8 else jnp.where(row8 < nb, v, -jnp.inf))
        eids.append(si0[a:a + 1] * N_KEYS + si1[0:8])
        flat.append(row8 + PEER_TOPK * a)
    vals.append(sv0[8:16] + sv1[0:1])
    eids.append(si0[8:16] * N_KEYS + si1[0:1])
    flat.append((row8 + 8) * PEER_TOPK)
    cand = jnp.concatenate(vals, axis=0)
    ce = jnp.concatenate(eids, axis=0)
    cf = jnp.concatenate(flat, axis=0)
    fv, fe = [], []
    for _ in range(PEER_TOPK):
        m = jnp.max(cand, axis=0, keepdims=True)
        pick = jnp.min(jnp.where(cand == m, cf, PEER_TOPK * PEER_TOPK), axis=0, keepdims=True)
        sel = cf == pick
        fe.append(jnp.max(jnp.where(sel, ce, -1), axis=0, keepdims=True))
        fv.append(m)
        cand = jnp.where(sel, -jnp.inf, cand)
    fv = jnp.concatenate(fv, axis=0)
    fe = jnp.concatenate(fe, axis=0)
    p = jnp.exp(fv - fv[0:1])
    return fe, p / jnp.sum(p, axis=0, keepdims=True)


def _peer_topk_kernel(q_ref, keys_ref, a_ref, b_ref, g_ref):
    q = q_ref[...].astype(BF16)
    es, gs = [], []
    for h in range(PEER_HEADS):
        tops = []
        for c in range(2):
            hc = 2 * h + c
            s = lax.dot_general(keys_ref[hc], q[:, hc * PEER_HALF:(hc + 1) * PEER_HALF], NT_DIMS,
                                preferred_element_type=F32)
            tops.append(_top16_rows(s))
        e, g = _peer_pairs(tops[0][0], tops[0][1], tops[1][0], tops[1][1])
        es.append(e)
        gs.append(g)
    e = jnp.concatenate(es, axis=0)
    g = jnp.concatenate(gs, axis=0)
    a_ref[...] = (e >> 7).astype(F32).T
    b_ref[...] = (e & (N_KEYS - 1)).astype(F32).T
    g_ref[...] = g.T


PEER_TOPK_TILE = 128


def peer_topk(q, keys_bf):
    n = q.shape[0]
    tt = PEER_TOPK_TILE
    spec = pl.BlockSpec((tt, PEER_PAIRS), lambda i: (i, 0))
    shp = jax.ShapeDtypeStruct((n, PEER_PAIRS), F32)
    return pl.pallas_call(
        _peer_topk_kernel,
        out_shape=(shp, shp, shp),
        grid=(n // tt,),
        in_specs=[pl.BlockSpec((tt, PEER_HEADS * PEER_KEY_DIM), lambda i: (i, 0)),
                  pl.BlockSpec((2 * PEER_HEADS, N_KEYS, PEER_HALF), lambda i: (0, 0, 0))],
        out_specs=(spec, spec, spec),
        compiler_params=pltpu.CompilerParams(dimension_semantics=("parallel",),
                                             vmem_limit_bytes=VMEM_LIMIT),
        name="peer_topk",
    )(q, keys_bf)


W_PITCH = 136
PEER_W_TILE = 64


def _peer_w_kernel(a_ref, b_ref, g_ref, o_ref, wf_ref):
    tt = a_ref.shape[0]
    ii = lax.broadcasted_iota(I32, (N_KEYS, PEER_PAIRS), 0).astype(F32)

    def body(t, carry):
        a = a_ref[pl.ds(t, 1), :]
        b = b_ref[pl.ds(t, 1), :]
        g = g_ref[pl.ds(t, 1), :]
        at = jnp.where(ii == a, 1.0, 0.0).astype(BF16)
        bt = jnp.where(ii == b, g, 0.0).astype(BF16)
        w = lax.dot_general(at, bt, NT_DIMS, preferred_element_type=F32)
        wf_ref[pl.ds(pl.multiple_of(t * W_PITCH, 8), N_KEYS), :] = w
        return carry

    lax.fori_loop(0, tt, body, 0, unroll=8)
    for i1 in range(N_KEYS):
        o_ref[:, i1 * N_KEYS:(i1 + 1) * N_KEYS] = wf_ref[pl.ds(i1, tt, stride=W_PITCH), :].astype(BF16)


def peer_gate_matrix(a, b, g):
    n = a.shape[0]
    tt = PEER_W_TILE
    spec = pl.BlockSpec((tt, PEER_PAIRS), lambda i: (i, 0))
    return pl.pallas_call(
        _peer_w_kernel,
        out_shape=jax.ShapeDtypeStruct((n, N_EXPERTS), BF16),
        grid=(n // tt,),
        in_specs=[spec, spec, spec],
        out_specs=pl.BlockSpec((tt, N_EXPERTS), lambda i: (i, 0)),
        scratch_shapes=[pltpu.VMEM((tt * W_PITCH, N_KEYS), F32)],
        compiler_params=pltpu.CompilerParams(dimension_semantics=("parallel",),
                                             vmem_limit_bytes=VMEM_LIMIT),
        name="peer_gate_matrix",
    )(a, b, g)


def _peer_dense_kernel(xn_ref, x_ref, ut_ref, w_ref, v_ref, o_ref):
    @pl.when(pl.program_id(1) == 0)
    def _():
        o_ref[...] = x_ref[...]

    h = jnp.dot(xn_ref[...], ut_ref[...], preferred_element_type=F32)
    act = 0.5 * h * (1.0 + lax.erf(h * np.float32(math.sqrt(0.5))))
    wa = (w_ref[...].astype(F32) * act).astype(BF16)
    o_ref[...] += jnp.dot(wa, v_ref[...], preferred_element_type=F32)


PEER_EXPERT_TILE = 1024


def peer_dense(xn, x, ut, w, v):
    n = x.shape[0]
    tm = _pick_tile(n, (1024, 512, 256, 128))
    te = PEER_EXPERT_TILE
    return pl.pallas_call(
        _peer_dense_kernel,
        out_shape=jax.ShapeDtypeStruct((n, D_MODEL), F32),
        grid=(n // tm, N_EXPERTS // te),
        in_specs=[pl.BlockSpec((tm, D_MODEL), lambda i, j: (i, 0)),
                  pl.BlockSpec((tm, D_MODEL), lambda i, j: (i, 0)),
                  pl.BlockSpec((D_MODEL, te), lambda i, j: (0, j)),
                  pl.BlockSpec((tm, te), lambda i, j: (i, j)),
                  pl.BlockSpec((te, D_MODEL), lambda i, j: (j, 0))],
        out_specs=pl.BlockSpec((tm, D_MODEL), lambda i, j: (i, 0)),
        compiler_params=pltpu.CompilerParams(dimension_semantics=("parallel", "arbitrary"),
                                             vmem_limit_bytes=VMEM_LIMIT),
        name="peer_dense",
    )(xn, x, ut, w, v)


def peer_block(x2, norm2_g, wq_bf, keys_bf, ut_bf, v_bf):
    n = x2.shape[0]
    npad = -(-n // PEER_TOPK_TILE) * PEER_TOPK_TILE
    xp = jnp.pad(x2, ((0, npad - n), (0, 0))) if npad != n else x2
    q, xn = norm_matmul(xp, norm2_g, wq_bf)
    a, b, g = peer_topk(q, keys_bf)
    w = peer_gate_matrix(a, b, g)
    out = peer_dense(xn, xp, ut_bf, w, v_bf)
    return out[:n] if npad != n else out


def _group_sum64(xx, bd):
    hi = xx.astype(BF16)
    r1 = xx - hi.astype(F32)
    mid = r1.astype(BF16)
    lo = (r1 - mid.astype(F32)).astype(BF16)
    dot = lambda a: jnp.dot(a, bd, preferred_element_type=F32)
    return dot(hi) + dot(mid) + dot(lo)


def _rope128(y, c, s, lo_half):
    partner = jnp.where(lo_half, pltpu.roll(y, LANES - 32, axis=1), pltpu.roll(y, 32, axis=1))
    return y * c + partner * s


def _attn_prep_kernel(q_ref, kv_ref, iq_ref, sm_ref, cos_ref, sin_ref, qg_ref, kg_ref,
                      qo_ref, ko_ref, vo_ref, iqo_ref, iko_ref):
    tm = q_ref.shape[0]
    lane = lax.broadcasted_iota(I32, (tm, LANES), 1)
    lo_half = (lane & (HEAD_DIM - 1)) < HEAD_DIM // 2
    grp_r = lax.broadcasted_iota(I32, (LANES, LANES), 0) >> 6
    grp_c = lax.broadcasted_iota(I32, (LANES, LANES), 1) >> 6
    bd = jnp.where(grp_r == grp_c, 1.0, 0.0).astype(BF16)
    c = cos_ref[...]
    s = sin_ref[...]

    def normed(x, g):
        ms = _group_sum64(x * x, bd) * np.float32(1.0 / HEAD_DIM)
        return x * lax.rsqrt(ms + EPS) * g

    for j in range(ATT_WIDTH // LANES):
        sl = slice(j * LANES, (j + 1) * LANES)
        qo_ref[:, sl] = _rope128(normed(q_ref[:, sl], qg_ref[...]), c, s, lo_half)
    for j in range(KV_WIDTH // LANES):
        sl = slice(j * LANES, (j + 1) * LANES)
        ko_ref[:, sl] = _rope128(normed(kv_ref[:, sl], kg_ref[...]), c, s, lo_half)
    vo_ref[...] = kv_ref[:, KV_WIDTH:2 * KV_WIDTH]
    for j in range(IDX_HEADS * IDX_DIM // LANES):
        sl = slice(j * LANES, (j + 1) * LANES)
        iqo_ref[:, sl] = _rope128(iq_ref[:, sl], c, s, lo_half)
    iko_ref[...] = _rope128(sm_ref[...], c, s, lo_half)[:, SMALL_IK:SMALL_IK + IDX_DIM]


def attn_prep(proj, cos_t, sin_t, q_norm_g, k_norm_g):
    n = proj.shape[0]
    tm = _pick_tile(n, (256, 128))
    nt = cos_t.shape[0] // tm
    row = lambda w, off: pl.BlockSpec((tm, w), lambda i: (i, off // w))
    tab = pl.BlockSpec((tm, LANES), lambda i: (i % nt, 0))
    gain = pl.BlockSpec((1, LANES), lambda i: (0, 0))
    out = lambda w: pl.BlockSpec((tm, w), lambda i: (i, 0))
    shp = lambda w: jax.ShapeDtypeStruct((n, w), F32)
    tile2 = lambda g: jnp.tile(g.reshape(1, HEAD_DIM), (1, LANES // HEAD_DIM))
    return pl.pallas_call(
        _attn_prep_kernel,
        out_shape=(shp(ATT_WIDTH), shp(KV_WIDTH), shp(KV_WIDTH), shp(IDX_HEADS * IDX_DIM), shp(IDX_DIM)),
        grid=(n // tm,),
        in_specs=[row(ATT_WIDTH, OFF_Q), row(2 * KV_WIDTH, OFF_KV), row(IDX_HEADS * IDX_DIM, OFF_IQ),
                  row(LANES, OFF_SMALL), tab, tab, gain, gain],
        out_specs=(out(ATT_WIDTH), out(KV_WIDTH), out(KV_WIDTH), out(IDX_HEADS * IDX_DIM), out(IDX_DIM)),
        compiler_params=pltpu.CompilerParams(dimension_semantics=("parallel",),
                                             vmem_limit_bytes=VMEM_LIMIT),
        name="attn_prep",
    )(proj, proj, proj, proj, cos_t, sin_t, tile2(q_norm_g), tile2(k_norm_g))


def rope_tables(pos):
    inv = ROPE_THETA ** (-jnp.arange(0, HEAD_DIM, 2, dtype=F32) / HEAD_DIM)
    ang = pos.astype(F32)[:, None] * inv[None, :]
    cos, sin = jnp.cos(ang), jnp.sin(ang)
    reps = LANES // HEAD_DIM
    return (jnp.tile(jnp.concatenate([cos, cos], axis=-1), (1, reps)),
            jnp.tile(jnp.concatenate([-sin, sin], axis=-1), (1, reps)))


SSD_HEAD_LANE = SMALL_DT
XS_W = D_INNER
BC_W = 2 * SSM_GROUPS * D_STATE
GROUP_W = SSM_HPG * SSM_HEAD_DIM
CONV_COL_CHUNK = 512


def _split3(x):
    hi = x.astype(BF16)
    r1 = x - hi.astype(F32)
    mid = r1.astype(BF16)
    return hi, mid, (r1 - mid.astype(F32)).astype(BF16)


def _ssd_kernel(z_ref, xs_ref, bc_ref, sm_ref, cw_ref, cb_ref, dtb_ref, alog_ref, dskip_ref, ng_ref,
                exp_ref, y_ref, h_ref, cn_ref, xp_ref, u_ref, ht_ref):
    c = pl.program_id(1)
    q = SSD_CHUNK

    @pl.when(c == 0)
    def _():
        xp_ref[0:8, :] = jnp.zeros((8, CONV_DIM), F32)
        ht_ref[...] = jnp.zeros(ht_ref.shape, F32)

    xp_ref[8:8 + q, 0:XS_W] = xs_ref[...]
    xp_ref[8:8 + q, XS_W:CONV_DIM] = bc_ref[...]
    for cc in range(CONV_DIM // CONV_COL_CHUNK):
        sl = slice(cc * CONV_COL_CHUNK, (cc + 1) * CONV_COL_CHUNK)
        conv = cb_ref[:, sl]
        for j in range(CONV_W):
            conv = conv + xp_ref[8 - (CONV_W - 1) + j:8 - (CONV_W - 1) + j + q, sl] * cw_ref[j:j + 1, sl]
        u_ref[:, sl] = conv * jax.nn.sigmoid(conv)

    @pl.when(c == pl.num_programs(1) - 1)
    def _():
        cn_ref[...] = xp_ref[8 + q - (CONV_W - 1):8 + q, :]

    xp_ref[0:8, :] = xp_ref[q:q + 8, :]

    lane = lax.broadcasted_iota(I32, (q, LANES), 1)
    head_lane = (lane >= SSD_HEAD_LANE) & (lane < SSD_HEAD_LANE + SSM_HEADS)
    pre = sm_ref[...] + dtb_ref[...]
    dt = jnp.where(head_lane, jnp.maximum(pre, 0.0) + jnp.log1p(jnp.exp(-jnp.abs(pre))), 0.0)
    ad = dt * (-jnp.exp(alog_ref[...]))
    row = lax.broadcasted_iota(I32, (q, q), 0)
    col = lax.broadcasted_iota(I32, (q, q), 1)
    tril = row >= col
    tri = jnp.where(tril, 1.0, 0.0).astype(BF16)
    acs = sum(jnp.dot(tri, part, preferred_element_type=F32) for part in _split3(ad))
    acs_last = acs[q - 1:q, :]
    expand = exp_ref[...]
    e1 = jnp.dot(jnp.exp(acs).astype(BF16), expand, preferred_element_type=F32)
    e2 = jnp.dot((jnp.exp(acs_last - acs) * dt).astype(BF16), expand, preferred_element_type=F32)
    dec8 = jnp.broadcast_to(jnp.exp(acs_last), (8, LANES))
    dec = sum(jnp.dot(part, expand, preferred_element_type=F32) for part in _split3(dec8))[0:1, :]
    acs_t = acs.T
    dt_t = dt.T
    lane_lo = lane < SSM_HEAD_DIM

    for g in range(SSM_GROUPS):
        gs = slice(g * GROUP_W, (g + 1) * GROUP_W)
        bg = u_ref[:, XS_W + g * D_STATE:XS_W + (g + 1) * D_STATE]
        cg = u_ref[:, XS_W + BC_W // 2 + g * D_STATE:XS_W + BC_W // 2 + (g + 1) * D_STATE].astype(BF16)
        cb = lax.dot_general(cg, bg.astype(BF16), NT_DIMS, preferred_element_type=F32)
        xs_g = u_ref[:, gs]
        y_diag = []
        for jp in range(SSM_HPG // 2):
            ms = []
            for r in (g * SSM_HPG + 2 * jp, g * SSM_HPG + 2 * jp + 1):
                hl = SSD_HEAD_LANE + r
                seg = acs[:, hl:hl + 1] - acs_t[hl:hl + 1, :]
                lmat = jnp.where(tril, jnp.exp(seg), 0.0)
                ms.append((cb * lmat * dt_t[hl:hl + 1, :]).astype(BF16))
            blk = xs_g[:, jp * LANES:(jp + 1) * LANES]
            rhs = jnp.concatenate([jnp.where(lane_lo, blk, 0.0), jnp.where(lane_lo, 0.0, blk)], axis=0)
            y_diag.append(jnp.dot(jnp.concatenate(ms, axis=1), rhs.astype(BF16), preferred_element_type=F32))
        h_old = ht_ref[:, gs]
        y_off = jnp.dot(cg, h_old.astype(BF16), preferred_element_type=F32) * e1[:, gs]
        y = jnp.concatenate(y_diag, axis=1) + y_off + dskip_ref[:, gs] * xs_g
        zz = z_ref[:, gs]
        y = y * (zz * jax.nn.sigmoid(zz))
        ms_y = jnp.sum(y * y, axis=-1, keepdims=True) * np.float32(1.0 / GROUP_W)
        y_ref[:, gs] = (y * lax.rsqrt(ms_y + EPS) * ng_ref[:, gs]).astype(BF16)
        upd = jnp.dot(bg.T.astype(BF16), (xs_g * e2[:, gs]).astype(BF16), preferred_element_type=F32)
        ht_ref[:, gs] = h_old * dec[:, gs] + upd

    @pl.when(c == pl.num_programs(1) - 1)
    def _():
        for j in range(D_INNER // LANES):
            h_ref[j * LANES:(j + 1) * LANES, :] = ht_ref[:, j * LANES:(j + 1) * LANES].T


def ssd_prompt(proj, conv_w, conv_b, dt_bias, a_log, d_skip, norm_g, bsz, seq):
    nc = seq // SSD_CHUNK
    n = bsz * seq
    rmap = lambda blk: (lambda b, c: (b * nc + c, blk))
    const = lambda a: pl.BlockSpec(a.shape, lambda b, c: (0, 0))
    head_row = lambda v: jnp.zeros((1, LANES), F32).at[0, SSD_HEAD_LANE:SSD_HEAD_LANE + SSM_HEADS].set(v)
    expand = (jnp.arange(LANES)[:, None] == SSD_HEAD_LANE + jnp.arange(D_INNER)[None, :] // SSM_HEAD_DIM).astype(BF16)
    consts = (conv_w, conv_b.reshape(1, CONV_DIM), head_row(dt_bias), head_row(a_log),
              jnp.repeat(d_skip, SSM_HEAD_DIM).reshape(1, D_INNER), norm_g.reshape(1, D_INNER), expand)
    return pl.pallas_call(
        _ssd_kernel,
        out_shape=(jax.ShapeDtypeStruct((n, D_INNER), BF16),
                   jax.ShapeDtypeStruct((bsz, D_INNER, D_STATE), F32),
                   jax.ShapeDtypeStruct((bsz, CONV_W - 1, CONV_DIM), F32)),
        grid=(bsz, nc),
        in_specs=[pl.BlockSpec((SSD_CHUNK, D_INNER), rmap(OFF_Z // D_INNER)),
                  pl.BlockSpec((SSD_CHUNK, XS_W), rmap(OFF_XBC // XS_W)),
                  pl.BlockSpec((SSD_CHUNK, BC_W), rmap((OFF_XBC + XS_W) // BC_W)),
                  pl.BlockSpec((SSD_CHUNK, LANES), rmap(OFF_SMALL // LANES))] + [const(a) for a in consts],
        out_specs=(pl.BlockSpec((SSD_CHUNK, D_INNER), rmap(0)),
                   pl.BlockSpec((None, D_INNER, D_STATE), lambda b, c: (b, 0, 0)),
                   pl.BlockSpec((None, CONV_W - 1, CONV_DIM), lambda b, c: (b, 0, 0))),
        scratch_shapes=[pltpu.VMEM((SSD_CHUNK + 8, CONV_DIM), F32),
                        pltpu.VMEM((SSD_CHUNK, CONV_DIM), F32),
                        pltpu.VMEM((D_STATE, D_INNER), F32)],
        compiler_params=pltpu.CompilerParams(dimension_semantics=("parallel", "arbitrary"),
                                             vmem_limit_bytes=VMEM_LIMIT),
        name="ssd_prompt",
    )(proj, proj, proj, proj, *consts)


def _mix_kernel(ya_ref, yb_ref, ga_ref, gb_ref, x_ref, wa_ref, wb_ref, wo_ref, o_ref):
    ma = jnp.dot(ya_ref[...].astype(BF16), wa_ref[...], preferred_element_type=F32)
    mb = jnp.dot(yb_ref[...].astype(BF16), wb_ref[...], preferred_element_type=F32)
    mixed = jax.nn.sigmoid(ga_ref[...]) * ma + jax.nn.sigmoid(gb_ref[...]) * mb
    o_ref[...] = x_ref[...] + jnp.dot(mixed.astype(BF16), wo_ref[...], preferred_element_type=F32)


def branch_mix(ya, yb, proj, x, wa, wb, wo):
    n = x.shape[0]
    tm = _pick_tile(n, (256, 128))
    rows = lambda w, blk=0: pl.BlockSpec((tm, w), lambda i: (i, blk))
    whole = lambda a: pl.BlockSpec(a.shape, lambda i: (0, 0))
    return pl.pallas_call(
        _mix_kernel,
        out_shape=jax.ShapeDtypeStruct((n, D_MODEL), F32),
        grid=(n // tm,),
        in_specs=[rows(D_INNER), rows(ATT_WIDTH), rows(D_MODEL, OFF_GA // D_MODEL),
                  rows(D_MODEL, OFF_GB // D_MODEL), rows(D_MODEL), whole(wa), whole(wb), whole(wo)],
        out_specs=rows(D_MODEL),
        compiler_params=pltpu.CompilerParams(dimension_semantics=("parallel",),
                                             vmem_limit_bytes=VMEM_LIMIT),
        name="branch_mix",
    )(ya, yb, proj, proj, x, wa, wb, wo)


def _dsa_prompt_body(s_len, qb, iq_ref, sm_ref, ik_ref, q_ref, k_ref, v_ref, o_ref, jc_ref):
    ik = ik_ref[0:s_len, :].astype(BF16)
    iw = sm_ref[:, SMALL_IW:SMALL_IW + IDX_HEADS] * np.float32(IDX_SCALE)
    score = jnp.zeros((Q_BLOCK, s_len), F32)
    for h in range(IDX_HEADS):
        iqh = iq_ref[:, h * IDX_DIM:(h + 1) * IDX_DIM].astype(BF16)
        d = lax.dot_general(iqh, ik, NT_DIMS, preferred_element_type=F32)
        score = score + jnp.maximum(d, 0.0) * iw[:, h:h + 1]

    tpos = qb * Q_BLOCK + lax.broadcasted_iota(I32, (Q_BLOCK, 1), 0)
    spos = lax.broadcasted_iota(I32, (1, s_len), 1)
    causal = spos <= tpos
    bits = lax.bitcast_convert_type(score, I32)
    key = bits ^ ((bits >> 31) & np.int32(0x7FFFFFFF))
    key = jnp.where(score == 0.0, 0, key)
    key = jnp.where(causal, key, INT_MIN)

    def count(mask):
        return jnp.sum(mask.astype(I32), axis=-1, keepdims=True)

    def bisect_value(i, tu):
        cand_u = tu | jnp.left_shift(np.int32(1), 31 - i)
        return jnp.where(count(key >= (cand_u ^ INT_MIN)) >= TOPK_MAX, cand_u, tu)

    thr = lax.fori_loop(0, 32, bisect_value, jnp.zeros((Q_BLOCK, 1), I32)) ^ INT_MIN
    above = key > thr
    tied = key == thr
    need = TOPK_MAX - count(above)
    excess = (count(tied) > need) & (thr != INT_MIN)

    jc_ref[...] = jnp.full((Q_BLOCK, LANES), s_len, I32)

    @pl.when(jnp.max(excess.astype(I32)) > 0)
    def _():
        def bisect_pos(i, j):
            cj = j | jnp.left_shift(np.int32(1), 10 - i)
            return jnp.where(count(tied & (spos < cj)) < need, cj, j)

        jc = lax.fori_loop(0, 11, bisect_pos, jnp.zeros((Q_BLOCK, 1), I32))
        jc_ref[...] = jnp.broadcast_to(jc, (Q_BLOCK, LANES))

    jc = jc_ref[:, 0:1]
    sel = causal & (above | (tied & (spos <= jc)))
    bias = jnp.where(sel, 0.0, -jnp.inf)

    for g in range(KV_HEADS):
        qs = jnp.concatenate(
            [q_ref[:, (ATT_REP * g + r) * HEAD_DIM:(ATT_REP * g + r + 1) * HEAD_DIM] for r in range(ATT_REP)],
            axis=0).astype(BF16)
        kg = k_ref[0:s_len, g * HEAD_DIM:(g + 1) * HEAD_DIM].astype(BF16)
        vg = v_ref[0:s_len, g * HEAD_DIM:(g + 1) * HEAD_DIM].astype(BF16)
        s = lax.dot_general(qs, kg, NT_DIMS, preferred_element_type=F32) * np.float32(HEAD_DIM ** -0.5)
        s = (s.reshape(ATT_REP, Q_BLOCK, s_len) + bias[None]).reshape(ATT_REP * Q_BLOCK, s_len)
        m = jnp.max(s, axis=-1, keepdims=True)
        p = jnp.exp(s - m)
        l = jnp.sum(p, axis=-1, keepdims=True)
        o = jnp.dot(p.astype(BF16), vg, preferred_element_type=F32) / l
        for r in range(ATT_REP):
            hh = ATT_REP * g + r
            o_ref[:, hh * HEAD_DIM:(hh + 1) * HEAD_DIM] = o[r * Q_BLOCK:(r + 1) * Q_BLOCK]


DSA_KEY_BUCKET = 512


def _dsa_prompt_kernel(*refs):
    qb = pl.program_id(1)
    s_full = refs[2].shape[0]
    per = DSA_KEY_BUCKET // Q_BLOCK
    for j in range(s_full // DSA_KEY_BUCKET):
        pl.when((qb >= per * j) & (qb < per * (j + 1)))(
            functools.partial(_dsa_prompt_body, DSA_KEY_BUCKET * (j + 1), qb, *refs))


def dsa_prompt(q, k, v, iq, ik, proj, bsz, seq):
    nq = seq // Q_BLOCK
    assert seq == 2048 and TOPK_MAX <= seq // 4
    qmap = lambda b, i: (b * nq + i, 0)
    bmap = lambda b, i: (b, 0)
    return pl.pallas_call(
        _dsa_prompt_kernel,
        out_shape=jax.ShapeDtypeStruct((bsz * seq, ATT_WIDTH), F32),
        grid=(bsz, nq),
        in_specs=[pl.BlockSpec((Q_BLOCK, IDX_HEADS * IDX_DIM), qmap),
                  pl.BlockSpec((Q_BLOCK, LANES), lambda b, i: (b * nq + i, OFF_SMALL // LANES)),
                  pl.BlockSpec((seq, IDX_DIM), bmap),
                  pl.BlockSpec((Q_BLOCK, ATT_WIDTH), qmap),
                  pl.BlockSpec((seq, KV_WIDTH), bmap),
                  pl.BlockSpec((seq, KV_WIDTH), bmap)],
        out_specs=pl.BlockSpec((Q_BLOCK, ATT_WIDTH), qmap),
        scratch_shapes=[pltpu.VMEM((Q_BLOCK, LANES), I32)],
        compiler_params=pltpu.CompilerParams(dimension_semantics=("parallel", "arbitrary"),
                                             vmem_limit_bytes=VMEM_LIMIT),
        name="dsa_prompt",
    )(iq, proj, ik, q, k, v)


def _column_tile(row):
    d = row.shape[1]
    eye = lax.broadcasted_iota(I32, (d, d), 0) == lax.broadcasted_iota(I32, (d, d), 1)
    col = jnp.sum(jnp.where(eye, jnp.broadcast_to(row, (d, d)), 0.0), axis=1, keepdims=True)
    return jnp.where(lax.broadcasted_iota(I32, (d, LANES), 1) == 0, col, 0.0)


def _dsa_sample_kernel(layer, n_pages, pt_ref, iq_ref, iw_ref, ikn_ref, q_ref, kn_ref, vn_ref,
                       cik_hbm, ck_hbm, cv_hbm, o_ref, ikb, kb, vb, sems):
    b = pl.program_id(0)
    past = n_pages * PAGE_SIZE
    width = past + LANES

    streams = ((cik_hbm, ikb), (ck_hbm, kb), (cv_hbm, vb))

    def page_copy(j, which):
        src, dst = streams[which]
        cols = pl.ds(pl.multiple_of(j * PAGE_SIZE, PAGE_SIZE), PAGE_SIZE)
        return pltpu.make_async_copy(src.at[layer, pt_ref[b, j]], dst.at[..., cols], sems.at[which])

    def start_page(j, carry):
        for which in range(len(streams)):
            page_copy(j, which).start()
        return carry

    lax.fori_loop(0, n_pages, start_page, 0)

    ikb[:, past:width] = _column_tile(ikn_ref[...])
    for g in range(KV_HEADS):
        kb[g, :, past:width] = _column_tile(kn_ref[:, g * HEAD_DIM:(g + 1) * HEAD_DIM])
        vb[g, :, past:width] = _column_tile(vn_ref[:, g * HEAD_DIM:(g + 1) * HEAD_DIM])

    def wait_stream(which):
        def wait_page(j, carry):
            page_copy(j, which).wait()
            return carry
        lax.fori_loop(0, n_pages, wait_page, 0)

    wait_stream(0)
    iq16 = jnp.concatenate([iq_ref[...], jnp.zeros((BF16_ROWS - IDX_HEADS, IDX_DIM), F32)], axis=0)
    sc = jnp.dot(iq16.astype(BF16), ikb[...].astype(BF16), preferred_element_type=F32)[0:IDX_HEADS]
    score = jnp.sum(jnp.maximum(sc, 0.0) * (iw_ref[...] * np.float32(IDX_SCALE)), axis=0, keepdims=True)
    pos = lax.broadcasted_iota(I32, (1, width), 1)
    allowed = pos <= past
    bits = lax.bitcast_convert_type(score, I32)
    key = bits ^ ((bits >> 31) & np.int32(0x7FFFFFFF))
    key = jnp.where(score == 0.0, 0, key)
    key = jnp.where(allowed, key, INT_MIN)

    def count(mask):
        return jnp.sum(mask.astype(I32), axis=-1, keepdims=True)

    def bisect_value(i, tu):
        cand_u = tu | jnp.left_shift(np.int32(1), 31 - i)
        return jnp.where(count(key >= (cand_u ^ INT_MIN)) >= TOPK_MAX, cand_u, tu)

    thr = lax.fori_loop(0, 32, bisect_value, jnp.zeros((1, 1), I32)) ^ INT_MIN
    above = key > thr
    tied = key == thr
    need = TOPK_MAX - count(above)
    pos_bits = int(width - 1).bit_length()

    def bisect_pos(i, j):
        cj = j | jnp.left_shift(np.int32(1), pos_bits - 1 - i)
        return jnp.where(count(tied & (pos < cj)) < need, cj, j)

    jc = lax.fori_loop(0, pos_bits, bisect_pos, jnp.zeros((1, 1), I32))
    sel = allowed & (above | (tied & (pos <= jc)))
    bias = jnp.where(sel, 0.0, -jnp.inf)

    wait_stream(1)
    wait_stream(2)
    for g in range(KV_HEADS):
        qg = jnp.concatenate(
            [q_ref[:, (ATT_REP * g + r) * HEAD_DIM:(ATT_REP * g + r + 1) * HEAD_DIM] for r in range(ATT_REP)]
            + [jnp.zeros((BF16_ROWS - ATT_REP, HEAD_DIM), F32)], axis=0).astype(BF16)
        s = jnp.dot(qg, kb[g].astype(BF16), preferred_element_type=F32) * np.float32(HEAD_DIM ** -0.5) + bias
        m = jnp.max(s, axis=-1, keepdims=True)
        p = jnp.exp(s - m)
        l = jnp.sum(p, axis=-1, keepdims=True)
        o = lax.dot_general(p.astype(BF16), vb[g].astype(BF16), NT_DIMS, preferred_element_type=F32) / l
        for r in range(ATT_REP):
            hh = ATT_REP * g + r
            o_ref[:, hh * HEAD_DIM:(hh + 1) * HEAD_DIM] = o[r:r + 1]


def dsa_sample(layer, q, k, v, iq, ik, iw, ck_t, cv_t, cik_t, page_table):
    bs, n_pages = page_table.shape
    width = n_pages * PAGE_SIZE + LANES
    assert TOPK_MAX <= (n_pages * PAGE_SIZE + 1) // 4
    row = lambda w: pl.BlockSpec((None, 1, w), lambda b, pt: (b, 0, 0))
    anyspec = pl.BlockSpec(memory_space=pl.ANY)
    grid_spec = pltpu.PrefetchScalarGridSpec(
        num_scalar_prefetch=1,
        grid=(bs,),
        in_specs=[pl.BlockSpec((None, IDX_HEADS, IDX_DIM), lambda b, pt: (b, 0, 0)),
                  pl.BlockSpec((None, IDX_HEADS, 1), lambda b, pt: (b, 0, 0)),
                  row(IDX_DIM), row(ATT_WIDTH), row(KV_WIDTH), row(KV_WIDTH),
                  anyspec, anyspec, anyspec],
        out_specs=row(ATT_WIDTH),
        scratch_shapes=[pltpu.VMEM((IDX_DIM, width), F32),
                        pltpu.VMEM((KV_HEADS, HEAD_DIM, width), F32),
                        pltpu.VMEM((KV_HEADS, HEAD_DIM, width), F32),
                        pltpu.SemaphoreType.DMA((3,))])
    out = pl.pallas_call(
        functools.partial(_dsa_sample_kernel, layer, n_pages),
        out_shape=jax.ShapeDtypeStruct((bs, 1, ATT_WIDTH), F32),
        grid_spec=grid_spec,
        compiler_params=pltpu.CompilerParams(dimension_semantics=("arbitrary",),
                                             vmem_limit_bytes=VMEM_LIMIT),
        name="dsa_sample",
    )(page_table, iq.reshape(bs, IDX_HEADS, IDX_DIM), iw.reshape(bs, IDX_HEADS, 1),
      ik.reshape(bs, 1, IDX_DIM), q.reshape(bs, 1, ATT_WIDTH), k.reshape(bs, 1, KV_WIDTH),
      v.reshape(bs, 1, KV_WIDTH), cik_t, ck_t, cv_t)
    return out.reshape(bs, ATT_WIDTH)


def _rms(x, g):
    xf = x.astype(F32)
    y = xf * lax.rsqrt(jnp.mean(xf * xf, axis=-1, keepdims=True) + EPS)
    return y * g.astype(F32)


def _rope(x, pos):
    d = x.shape[-1]
    inv = ROPE_THETA ** (-jnp.arange(0, d, 2, dtype=F32) / d)
    ang = pos.astype(F32)[:, None] * inv[None, :]
    ang = ang.reshape((ang.shape[0],) + (1,) * (x.ndim - 3) + (d // 2,))
    cos, sin = jnp.cos(ang), jnp.sin(ang)
    x1, x2 = x[..., :d // 2], x[..., d // 2:]
    return jnp.concatenate([x1 * cos - x2 * sin, x2 * cos + x1 * sin], axis=-1)


def _gather_rows(rows, idx):
    return jax.vmap(lambda r, i: r[i])(rows, idx)


def _ssd_chunked(xs, dt, a, bm, cm, h0):
    bsz, seq = xs.shape[:2]
    nc = seq // SSD_CHUNK

    def to_chunks(t):
        return jnp.moveaxis(t.reshape((bsz, nc, SSD_CHUNK) + t.shape[2:]), 1, 0)

    xd = xs * dt[..., None]
    ad = dt * a
    causal = jnp.tril(jnp.ones((SSD_CHUNK, SSD_CHUNK), dtype=bool))

    def step(h, inp):
        xd_c, ad_c, b_c, c_c = inp
        acs = jnp.cumsum(ad_c, axis=1)
        seg = acs[:, :, None] - acs[:, None, :]
        lmat = jnp.exp(jnp.where(causal[None, :, :, None, None], seg, -jnp.inf))
        cb = jnp.einsum('blgn,bsgn->blsg', c_c, b_c)
        y_diag = jnp.einsum('blsg,blsgr,bsgrp->blgrp', cb, lmat, xd_c)
        y_off = jnp.einsum('blgn,bgrpn,blgr->blgrp', c_c, h, jnp.exp(acs))
        decay_end = jnp.exp(acs[:, -1:] - acs)
        h_new = h * jnp.exp(acs[:, -1])[..., None, None] + jnp.einsum('bsgn,bsgr,bsgrp->bgrpn', b_c, decay_end, xd_c)
        return h_new, y_diag + y_off

    h, ys = lax.scan(step, h0, (to_chunks(xd), to_chunks(ad), to_chunks(bm), to_chunks(cm)))
    y = jnp.moveaxis(ys, 0, 1).reshape(xs.shape)
    return y, h


def _ssd_recurrent(xs, dt, a, bm, cm, h0):
    def step(h, inp):
        x_t, dt_t, b_t, c_t = inp
        h = h * jnp.exp(dt_t * a)[..., None, None] + jnp.einsum('bgrp,bgn->bgrpn', x_t * dt_t[..., None], b_t)
        return h, jnp.einsum('bgn,bgrpn->bgrp', c_t, h)

    tf = lambda t: jnp.moveaxis(t, 1, 0)
    h, ys = lax.scan(step, h0, (tf(xs), tf(dt), tf(bm), tf(cm)))
    return jnp.moveaxis(ys, 0, 1), h


def _mamba2(z, xbc, dt_raw, conv_prefix, h0, conv_w, conv_b, dt_bias, a_log, d_skip, norm_g, chunked):
    bsz, t_len, _ = z.shape
    xp = jnp.concatenate([conv_prefix, xbc], axis=1)
    conv = conv_b + sum(xp[:, j:j + t_len] * conv_w[j] for j in range(CONV_W))
    conv_new = xp[:, t_len:]
    xbc = jax.nn.silu(conv)
    xs, bm, cm = jnp.split(xbc, [D_INNER, D_INNER + SSM_GROUPS * D_STATE], axis=-1)
    xs = xs.reshape(bsz, t_len, SSM_GROUPS, SSM_HPG, SSM_HEAD_DIM)
    bm = bm.reshape(bsz, t_len, SSM_GROUPS, D_STATE)
    cm = cm.reshape(bsz, t_len, SSM_GROUPS, D_STATE)
    dt = jax.nn.softplus(dt_raw + dt_bias).reshape(bsz, t_len, SSM_GROUPS, SSM_HPG)
    a = -jnp.exp(a_log).reshape(SSM_GROUPS, SSM_HPG)
    scan_fn = _ssd_chunked if chunked else _ssd_recurrent
    y, h = scan_fn(xs, dt, a, bm, cm, h0)
    y = y + d_skip.reshape(SSM_GROUPS, SSM_HPG, 1) * xs
    y = y.reshape(bsz, t_len, D_INNER) * jax.nn.silu(z)
    y = _rms(y.reshape(bsz, t_len, SSM_GROUPS, D_INNER // SSM_GROUPS), norm_g.reshape(SSM_GROUPS, -1))
    return y.reshape(bsz, t_len, D_INNER), h, conv_new


def _indexer_scores(iq, ik, iw):
    dots = jax.nn.relu(jnp.einsum('bthd,bsd->bths', iq, ik))
    return jnp.einsum('bths,bth->bts', dots, iw * IDX_SCALE)


def _sparse_attend(q, ks, vs, valid):
    bsz, t_len = q.shape[:2]
    qg = q.reshape(bsz, t_len, KV_HEADS, ATT_REP, HEAD_DIM)
    s = jnp.einsum('btgrd,btkgd->btgrk', qg, ks) * (HEAD_DIM ** -0.5)
    s = jnp.where(valid[:, :, None, None, :], s, -jnp.inf)
    p = jax.nn.softmax(s, axis=-1)
    o = jnp.einsum('btgrk,btkgd->btgrd', p, vs)
    return o.reshape(bsz, t_len, ATT_WIDTH)


def _dsa_sample(q, k, v, iq, ik, iw, ck, cv, cik, page_table):
    bsz, t_len = q.shape[:2]
    past = page_table.shape[1] * PAGE_SIZE
    n_keys = past + t_len
    topk = min(TOPK_MAX, n_keys // 4)
    ik_past = cik[page_table].reshape(bsz, past, IDX_DIM)
    score = _indexer_scores(iq, jnp.concatenate([ik_past, ik], axis=1), iw)
    qpos = past + jnp.arange(t_len)
    allowed = jnp.arange(n_keys)[None, :] <= qpos[:, None]
    score = jnp.where(allowed[None], score, -jnp.inf)
    _, idx = lax.top_k(score, topk)
    valid = idx <= qpos[None, :, None]
    from_new = idx >= past
    p_idx = jnp.minimum(idx, past - 1)
    phys = jax.vmap(lambda pt, i: pt[i // PAGE_SIZE])(page_table, p_idx)
    off = p_idx % PAGE_SIZE
    n_idx = jnp.clip(idx - past, 0, t_len - 1)

    def select(cache, new):
        return jnp.where(from_new[..., None, None], _gather_rows(new, n_idx), cache[phys, off])

    return _sparse_attend(q, select(ck, k), select(cv, v), valid)


def _layer(x, pos, lw, conv_prefix, h0, paged):
    (norm1_g, w_in_bf, conv_w, conv_b, dt_bias, a_log, d_skip, ssm_norm_g, q_norm_g, k_norm_g,
     w_a_bf, w_b_bf, w_out_bf, norm2_g, wq_bf, keys_bf, ut_bf, v_bf) = lw
    bsz, t_len, _ = x.shape
    n = bsz * t_len
    x2 = x.reshape(n, D_MODEL)
    proj, _ = norm_matmul(x2, norm1_g, w_in_bf)
    cos_t, sin_t = rope_tables(pos)
    q, k, v, iq, ik = attn_prep(proj, cos_t, sin_t, q_norm_g, k_norm_g)
    seg = lambda off, w: proj[:, off:off + w].reshape(bsz, t_len, w)
    if paged is None:
        y_a, h_new, conv_new = ssd_prompt(proj, conv_w, conv_b, dt_bias, a_log, d_skip, ssm_norm_g, bsz, t_len)
    else:
        y_a, h_new, conv_new = _mamba2(seg(OFF_Z, D_INNER), seg(OFF_XBC, CONV_DIM),
                                       seg(OFF_SMALL + SMALL_DT, SSM_HEADS), conv_prefix, h0, conv_w, conv_b,
                                       dt_bias, a_log, d_skip, ssm_norm_g, False)
    k4 = k.reshape(bsz, t_len, KV_HEADS, HEAD_DIM)
    v4 = v.reshape(bsz, t_len, KV_HEADS, HEAD_DIM)
    ik3 = ik.reshape(bsz, t_len, IDX_DIM)
    if paged is None:
        y_b = dsa_prompt(q, k, v, iq, ik, proj, bsz, t_len)
    else:
        assert t_len == 1
        y_b = dsa_sample(paged[0], q, k, v, iq, ik, proj[:, OFF_SMALL + SMALL_IW:OFF_SMALL + SMALL_IW + IDX_HEADS],
                         *paged[1:])
    x2 = branch_mix(y_a.reshape(n, D_INNER), y_b, proj, x2, w_a_bf, w_b_bf, w_out_bf)
    x2 = peer_block(x2, norm2_g, wq_bf, keys_bf, ut_bf, v_bf)
    h_out = h_new.reshape(bsz, SSM_HEADS, SSM_HEAD_DIM, D_STATE)
    return x2.reshape(bsz, t_len, D_MODEL), k4, v4, ik3, h_out, conv_new


def kernel(x_prompt, x_sample, cache_k, cache_v, cache_idx_k, state_ssm, state_conv, page_table,
           norm1_g, w_in, conv_w, conv_b, dt_bias, a_log, d_skip, ssm_norm_g, q_norm_g, k_norm_g,
           w_branch_a, w_branch_b, w_out, norm2_g, peer_wq, peer_keys, peer_u, peer_v):
    bp, seq = x_prompt.shape[:2]
    bs, t_new = x_sample.shape[:2]
    past = page_table.shape[1] * PAGE_SIZE
    pos_p = jnp.arange(seq)
    pos_s = jnp.tile(past + jnp.arange(t_new), bs)
    yp, ys = x_prompt, x_sample
    outs_p, outs_s = [], []
    ck_t = jnp.transpose(cache_k, (0, 1, 3, 4, 2))
    cv_t = jnp.transpose(cache_v, (0, 1, 3, 4, 2))
    cik_t = jnp.transpose(cache_idx_k, (0, 1, 3, 2))
    sizes = dict(zip(("z", "xbc", "dt", "q", "k", "v", "iq", "ik", "iw", "ga", "gb"), IN_SIZES))
    starts = dict(zip(sizes, np.cumsum((0,) + IN_SIZES[:-1])))
    order = ("z", "xbc", "q", "k", "v", "iq", "ga", "gb", "ik", "dt", "iw")
    for l in range(DEPTH):
        w_bf = w_in[l].astype(BF16)
        cols = [w_bf[:, int(starts[s]):int(starts[s]) + sizes[s]] for s in order]
        used = sum(sizes.values())
        w_in_pad = jnp.concatenate(cols + [jnp.zeros((D_MODEL, IN_WIDTH_PAD - used), BF16)], axis=1)
        lw = (norm1_g[l], w_in_pad, conv_w[l], conv_b[l], dt_bias[l], a_log[l], d_skip[l],
              ssm_norm_g[l], q_norm_g[l], k_norm_g[l], w_branch_a[l].astype(BF16),
              w_branch_b[l].astype(BF16), w_out[l].astype(BF16), norm2_g[l],
              peer_wq[l].astype(BF16),
              peer_keys[l].astype(BF16).reshape(2 * PEER_HEADS, N_KEYS, PEER_HALF),
              peer_u[l].T.astype(BF16), peer_v[l].astype(BF16))
        conv0 = jnp.zeros((bp, CONV_W - 1, CONV_DIM), F32)
        h0 = jnp.zeros((bp, SSM_GROUPS, SSM_HPG, SSM_HEAD_DIM, D_STATE), F32)
        yp, *rest = _layer(yp, pos_p, lw, conv0, h0, None)
        outs_p.append(rest)
        h_s = state_ssm[l].reshape(bs, SSM_GROUPS, SSM_HPG, SSM_HEAD_DIM, D_STATE)
        ys, *rest = _layer(ys, pos_s, lw, state_conv[l], h_s, (l, ck_t, cv_t, cik_t, page_table))
        outs_s.append(rest)
    stack = lambda outs, i: jnp.stack([o[i] for o in outs])
    return (yp, ys,
            stack(outs_p, 0), stack(outs_p, 1), stack(outs_p, 2), stack(outs_p, 3), stack(outs_p, 4),
            stack(outs_s, 0), stack(outs_s, 1), stack(outs_s, 2), stack(outs_s, 3), stack(outs_s, 4))
```

```python
import functools
import math

import jax
import jax.numpy as jnp
import numpy as np
from jax import lax
from jax.experimental import pallas as pl
from jax.experimental.pallas import tpu as pltpu

D_MODEL = 1024
DEPTH = 2
PAGE_SIZE = 128
D_INNER = 2048
SSM_HEAD_DIM = 64
SSM_HEADS = 32
SSM_GROUPS = 4
SSM_HPG = 8
D_STATE = 128
CONV_W = 4
CONV_DIM = D_INNER + 2 * SSM_GROUPS * D_STATE
SSD_CHUNK = 128
ATT_HEADS = 16
KV_HEADS = 4
HEAD_DIM = 64
ATT_REP = 4
ATT_WIDTH = 1024
KV_WIDTH = 256
IDX_HEADS = 8
IDX_DIM = 64
IDX_SCALE = (IDX_HEADS * IDX_DIM) ** -0.5
TOPK_MAX = 256
Q_BLOCK = 128
ROPE_THETA = 10000.0
PEER_HEADS = 8
N_KEYS = 128
N_EXPERTS = N_KEYS * N_KEYS
PEER_KEY_DIM = 256
PEER_HALF = 128
PEER_TOPK = 16
PEER_PAIRS = PEER_HEADS * PEER_TOPK
EPS = 1e-6
IN_SIZES = (D_INNER, CONV_DIM, SSM_HEADS, ATT_WIDTH, KV_WIDTH, KV_WIDTH,
            IDX_HEADS * IDX_DIM, IDX_DIM, IDX_HEADS, D_MODEL, D_MODEL)
IN_WIDTH = sum(IN_SIZES)
OFF_Z, OFF_XBC, OFF_Q, OFF_KV, OFF_IQ, OFF_GA, OFF_GB, OFF_SMALL = 0, 2048, 5120, 6144, 6656, 7168, 8192, 9216
SMALL_IK, SMALL_DT, SMALL_IW = 0, 64, 96
IN_WIDTH_PAD = 9728

F32 = jnp.float32
BF16 = jnp.bfloat16
I32 = jnp.int32
INT_MIN = np.int32(-2 ** 31)
VMEM_LIMIT = 56 * 1024 * 1024
LANES = 128
BF16_ROWS = 16
NT_DIMS = (((1,), (1,)), ((), ()))


def _tree_sum(x, axis):
    step = 8 if axis == 0 else LANES
    size = x.shape[axis]
    take = (lambda i: x[i:i + step, :]) if axis == 0 else (lambda i: x[:, i:i + step])
    parts = [take(i) for i in range(0, size, step)]
    while len(parts) > 1:
        pairs = [parts[i] + parts[i + 1] for i in range(0, len(parts) - 1, 2)]
        parts = pairs + ([parts[-1]] if len(parts) % 2 else [])
    return jnp.sum(parts[0], axis=axis, keepdims=True)


def _pick_tile(n, candidates):
    for c in candidates:
        if n % c == 0:
            return c
    return n


def _mm_norm_kernel(x_ref, g_ref, w_ref, o_ref, xn_ref):
    @pl.when(pl.program_id(1) == 0)
    def _():
        x = x_ref[...]
        ms = jnp.mean(x * x, axis=-1, keepdims=True)
        xn_ref[...] = (x * lax.rsqrt(ms + EPS) * g_ref[...]).astype(BF16)

    o_ref[...] = jnp.dot(xn_ref[...], w_ref[...], preferred_element_type=F32)


def norm_matmul(x, gain, w):
    m, k = x.shape
    n = w.shape[1]
    tm = _pick_tile(m, (2048, 1024, 512, 256, 128))
    tn = _pick_tile(n, (512, 256, 128))
    return pl.pallas_call(
        _mm_norm_kernel,
        out_shape=(jax.ShapeDtypeStruct((m, n), F32), jax.ShapeDtypeStruct((m, k), BF16)),
        grid=(m // tm, n // tn),
        in_specs=[pl.BlockSpec((tm, k), lambda i, j: (i, 0)),
                  pl.BlockSpec((1, k), lambda i, j: (0, 0)),
                  pl.BlockSpec((k, tn), lambda i, j: (0, j))],
        out_specs=(pl.BlockSpec((tm, tn), lambda i, j: (i, j)),
                   pl.BlockSpec((tm, k), lambda i, j: (i, 0))),
        compiler_params=pltpu.CompilerParams(dimension_semantics=("parallel", "arbitrary"),
                                             vmem_limit_bytes=VMEM_LIMIT),
        name="norm_matmul",
    )(x, gain.reshape(1, k), w)


def _top16_rows(s):
    r = s.shape[0]
    rows = lax.broadcasted_iota(I32, s.shape, 0)
    vals, idxs = [], []
    for _ in range(PEER_TOPK):
        m = jnp.max(s, axis=0, keepdims=True)
        i = jnp.min(jnp.where(s == m, rows, r), axis=0, keepdims=True)
        vals.append(m)
        idxs.append(i)
        s = jnp.where(rows == i, -jnp.inf, s)
    return jnp.concatenate(vals, axis=0), jnp.concatenate(idxs, axis=0)


def _peer_pairs(sv0, si0, sv1, si1):
    t = sv0.shape[1]
    row8 = lax.broadcasted_iota(I32, (8, t), 0)
    row16 = lax.broadcasted_iota(I32, (16, t), 0)
    vals = [sv0[0:1] + sv1]
    eids = [si0[0:1] * N_KEYS + si1]
    flat = [row16]
    for a in range(1, 8):
        nb = PEER_TOPK // (a + 1)
        v = sv0[a:a + 1] + sv1[0:8]
        vals.append(v if nb >= 8 else jnp.where(row8 < nb, v, -jnp.inf))
        eids.append(si0[a:a + 1] * N_KEYS + si1[0:8])
        flat.append(row8 + PEER_TOPK * a)
    vals.append(sv0[8:16] + sv1[0:1])
    eids.append(si0[8:16] * N_KEYS + si1[0:1])
    flat.append((row8 + 8) * PEER_TOPK)
    cand = jnp.concatenate(vals, axis=0)
    ce = jnp.concatenate(eids, axis=0)
    cf = jnp.concatenate(flat, axis=0)
    fv, fe = [], []
    for _ in range(PEER_TOPK):
        m = jnp.max(cand, axis=0, keepdims=True)
        pick = jnp.min(jnp.where(cand == m, cf, PEER_TOPK * PEER_TOPK), axis=0, keepdims=True)
        sel = cf == pick
        fe.append(jnp.max(jnp.where(sel, ce, -1), axis=0, keepdims=True))
        fv.append(m)
        cand = jnp.where(sel, -jnp.inf, cand)
    fv = jnp.concatenate(fv, axis=0)
    fe = jnp.concatenate(fe, axis=0)
    p = jnp.exp(fv - fv[0:1])
    return fe, p / jnp.sum(p, axis=0, keepdims=True)


def _peer_topk_kernel(q_ref, keys_ref, a_ref, b_ref, g_ref):
    q = q_ref[...].astype(BF16)
    es, gs = [], []
    for h in range(PEER_HEADS):
        tops = []
        for c in range(2):
            hc = 2 * h + c
            s = lax.dot_general(keys_ref[hc], q[:, hc * PEER_HALF:(hc + 1) * PEER_HALF], NT_DIMS,
                                preferred_element_type=F32)
            tops.append(_top16_rows(s))
        e, g = _peer_pairs(tops[0][0], tops[0][1], tops[1][0], tops[1][1])
        es.append(e)
        gs.append(g)
    e = jnp.concatenate(es, axis=0)
    g = jnp.concatenate(gs, axis=0)
    a_ref[...] = (e >> 7).astype(F32).T
    b_ref[...] = (e & (N_KEYS - 1)).astype(F32).T
    g_ref[...] = g.T


PEER_TOPK_TILE = 128


def peer_topk(q, keys_bf):
    n = q.shape[0]
    tt = PEER_TOPK_TILE
    spec = pl.BlockSpec((tt, PEER_PAIRS), lambda i: (i, 0))
    shp = jax.ShapeDtypeStruct((n, PEER_PAIRS), F32)
    return pl.pallas_call(
        _peer_topk_kernel,
        out_shape=(shp, shp, shp),
        grid=(n // tt,),
        in_specs=[pl.BlockSpec((tt, PEER_HEADS * PEER_KEY_DIM), lambda i: (i, 0)),
                  pl.BlockSpec((2 * PEER_HEADS, N_KEYS, PEER_HALF), lambda i: (0, 0, 0))],
        out_specs=(spec, spec, spec),
        compiler_params=pltpu.CompilerParams(dimension_semantics=("parallel",),
                                             vmem_limit_bytes=VMEM_LIMIT),
        name="peer_topk",
    )(q, keys_bf)


W_PITCH = 136
PEER_W_TILE = 64


def _peer_w_kernel(a_ref, b_ref, g_ref, o_ref, wf_ref):
    tt = a_ref.shape[0]
    ii = lax.broadcasted_iota(I32, (N_KEYS, PEER_PAIRS), 0).astype(F32)

    def body(t, carry):
        a = a_ref[pl.ds(t, 1), :]
        b = b_ref[pl.ds(t, 1), :]
        g = g_ref[pl.ds(t, 1), :]
        at = jnp.where(ii == a, 1.0, 0.0).astype(BF16)
        bt = jnp.where(ii == b, g, 0.0).astype(BF16)
        w = lax.dot_general(at, bt, NT_DIMS, preferred_element_type=F32)
        wf_ref[pl.ds(pl.multiple_of(t * W_PITCH, 8), N_KEYS), :] = w
        return carry

    lax.fori_loop(0, tt, body, 0, unroll=16)
    for i1 in range(N_KEYS):
        o_ref[:, i1 * N_KEYS:(i1 + 1) * N_KEYS] = wf_ref[pl.ds(i1, tt, stride=W_PITCH), :].astype(BF16)


def peer_gate_matrix(a, b, g):
    n = a.shape[0]
    tt = PEER_W_TILE
    spec = pl.BlockSpec((tt, PEER_PAIRS), lambda i: (i, 0))
    return pl.pallas_call(
        _peer_w_kernel,
        out_shape=jax.ShapeDtypeStruct((n, N_EXPERTS), BF16),
        grid=(n // tt,),
        in_specs=[spec, spec, spec],
        out_specs=pl.BlockSpec((tt, N_EXPERTS), lambda i: (i, 0)),
        scratch_shapes=[pltpu.VMEM((tt * W_PITCH, N_KEYS), F32)],
        compiler_params=pltpu.CompilerParams(dimension_semantics=("parallel",),
                                             vmem_limit_bytes=VMEM_LIMIT),
        name="peer_gate_matrix",
    )(a, b, g)


def _peer_dense_kernel(xn_ref, x_ref, ut_ref, w_ref, v_ref, o_ref):
    @pl.when(pl.program_id(1) == 0)
    def _():
        o_ref[...] = x_ref[...]

    h = jnp.dot(xn_ref[...], ut_ref[...], preferred_element_type=F32)
    act = 0.5 * h * (1.0 + lax.erf(h * np.float32(math.sqrt(0.5))))
    wa = (w_ref[...].astype(F32) * act).astype(BF16)
    o_ref[...] += jnp.dot(wa, v_ref[...], preferred_element_type=F32)


PEER_EXPERT_TILE = 1024


def peer_dense(xn, x, ut, w, v):
    n = x.shape[0]
    tm = _pick_tile(n, (1024, 512, 256, 128))
    te = PEER_EXPERT_TILE
    return pl.pallas_call(
        _peer_dense_kernel,
        out_shape=jax.ShapeDtypeStruct((n, D_MODEL), F32),
        grid=(n // tm, N_EXPERTS // te),
        in_specs=[pl.BlockSpec((tm, D_MODEL), lambda i, j: (i, 0)),
                  pl.BlockSpec((tm, D_MODEL), lambda i, j: (i, 0)),
                  pl.BlockSpec((D_MODEL, te), lambda i, j: (0, j)),
                  pl.BlockSpec((tm, te), lambda i, j: (i, j)),
                  pl.BlockSpec((te, D_MODEL), lambda i, j: (j, 0))],
        out_specs=pl.BlockSpec((tm, D_MODEL), lambda i, j: (i, 0)),
        compiler_params=pltpu.CompilerParams(dimension_semantics=("parallel", "arbitrary"),
                                             vmem_limit_bytes=VMEM_LIMIT),
        name="peer_dense",
    )(xn, x, ut, w, v)


def peer_block(x2, norm2_g, wq_bf, keys_bf, ut_bf, v_bf):
    n = x2.shape[0]
    npad = -(-n // PEER_TOPK_TILE) * PEER_TOPK_TILE
    xp = jnp.pad(x2, ((0, npad - n), (0, 0))) if npad != n else x2
    q, xn = norm_matmul(xp, norm2_g, wq_bf)
    a, b, g = peer_topk(q, keys_bf)
    w = peer_gate_matrix(a, b, g)
    out = peer_dense(xn, xp, ut_bf, w, v_bf)
    return out[:n] if npad != n else out


def _group_sum64(xx, bd):
    hi = xx.astype(BF16)
    r1 = xx - hi.astype(F32)
    mid = r1.astype(BF16)
    lo = (r1 - mid.astype(F32)).astype(BF16)
    dot = lambda a: jnp.dot(a, bd, preferred_element_type=F32)
    return dot(hi) + dot(mid) + dot(lo)


def _rope128(y, c, s, lo_half):
    partner = jnp.where(lo_half, pltpu.roll(y, LANES - 32, axis=1), pltpu.roll(y, 32, axis=1))
    return y * c + partner * s


def _attn_prep_kernel(q_ref, kv_ref, iq_ref, sm_ref, cos_ref, sin_ref, qg_ref, kg_ref,
                      qo_ref, ko_ref, vo_ref, iqo_ref, iko_ref):
    tm = q_ref.shape[0]
    lane = lax.broadcasted_iota(I32, (tm, LANES), 1)
    lo_half = (lane & (HEAD_DIM - 1)) < HEAD_DIM // 2
    grp_r = lax.broadcasted_iota(I32, (LANES, LANES), 0) >> 6
    grp_c = lax.broadcasted_iota(I32, (LANES, LANES), 1) >> 6
    bd = jnp.where(grp_r == grp_c, 1.0, 0.0).astype(BF16)
    c = cos_ref[...]
    s = sin_ref[...]

    def normed(x, g):
        ms = _group_sum64(x * x, bd) * np.float32(1.0 / HEAD_DIM)
        return x * lax.rsqrt(ms + EPS) * g

    for j in range(ATT_WIDTH // LANES):
        sl = slice(j * LANES, (j + 1) * LANES)
        qo_ref[:, sl] = _rope128(normed(q_ref[:, sl], qg_ref[...]), c, s, lo_half)
    for j in range(KV_WIDTH // LANES):
        sl = slice(j * LANES, (j + 1) * LANES)
        ko_ref[:, sl] = _rope128(normed(kv_ref[:, sl], kg_ref[...]), c, s, lo_half)
    vo_ref[...] = kv_ref[:, KV_WIDTH:2 * KV_WIDTH]
    for j in range(IDX_HEADS * IDX_DIM // LANES):
        sl = slice(j * LANES, (j + 1) * LANES)
        iqo_ref[:, sl] = _rope128(iq_ref[:, sl], c, s, lo_half)
    iko_ref[...] = _rope128(sm_ref[...], c, s, lo_half)[:, SMALL_IK:SMALL_IK + IDX_DIM]


def attn_prep(proj, cos_t, sin_t, q_norm_g, k_norm_g):
    n = proj.shape[0]
    tm = _pick_tile(n, (256, 128))
    nt = cos_t.shape[0] // tm
    row = lambda w, off: pl.BlockSpec((tm, w), lambda i: (i, off // w))
    tab = pl.BlockSpec((tm, LANES), lambda i: (i % nt, 0))
    gain = pl.BlockSpec((1, LANES), lambda i: (0, 0))
    out = lambda w: pl.BlockSpec((tm, w), lambda i: (i, 0))
    shp = lambda w: jax.ShapeDtypeStruct((n, w), F32)
    tile2 = lambda g: jnp.tile(g.reshape(1, HEAD_DIM), (1, LANES // HEAD_DIM))
    return pl.pallas_call(
        _attn_prep_kernel,
        out_shape=(shp(ATT_WIDTH), shp(KV_WIDTH), shp(KV_WIDTH), shp(IDX_HEADS * IDX_DIM), shp(IDX_DIM)),
        grid=(n // tm,),
        in_specs=[row(ATT_WIDTH, OFF_Q), row(2 * KV_WIDTH, OFF_KV), row(IDX_HEADS * IDX_DIM, OFF_IQ),
                  row(LANES, OFF_SMALL), tab, tab, gain, gain],
        out_specs=(out(ATT_WIDTH), out(KV_WIDTH), out(KV_WIDTH), out(IDX_HEADS * IDX_DIM), out(IDX_DIM)),
        compiler_params=pltpu.CompilerParams(dimension_semantics=("parallel",),
                                             vmem_limit_bytes=VMEM_LIMIT),
        name="attn_prep",
    )(proj, proj, proj, proj, cos_t, sin_t, tile2(q_norm_g), tile2(k_norm_g))


def rope_tables(pos):
    inv = ROPE_THETA ** (-jnp.arange(0, HEAD_DIM, 2, dtype=F32) / HEAD_DIM)
    ang = pos.astype(F32)[:, None] * inv[None, :]
    cos, sin = jnp.cos(ang), jnp.sin(ang)
    reps = LANES // HEAD_DIM
    return (jnp.tile(jnp.concatenate([cos, cos], axis=-1), (1, reps)),
            jnp.tile(jnp.concatenate([-sin, sin], axis=-1), (1, reps)))


SSD_HEAD_LANE = SMALL_DT
XS_W = D_INNER
BC_W = 2 * SSM_GROUPS * D_STATE
GROUP_W = SSM_HPG * SSM_HEAD_DIM
CONV_COL_CHUNK = 512


def _split3(x):
    hi = x.astype(BF16)
    r1 = x - hi.astype(F32)
    mid = r1.astype(BF16)
    return hi, mid, (r1 - mid.astype(F32)).astype(BF16)


def _ssd_kernel(z_ref, xs_ref, bc_ref, sm_ref, cw_ref, cb_ref, dtb_ref, alog_ref, dskip_ref, ng_ref,
                exp_ref, y_ref, h_ref, cn_ref, xp_ref, u_ref, ht_ref):
    c = pl.program_id(1)
    q = SSD_CHUNK

    @pl.when(c == 0)
    def _():
        xp_ref[0:8, :] = jnp.zeros((8, CONV_DIM), F32)
        ht_ref[...] = jnp.zeros(ht_ref.shape, F32)

    xp_ref[8:8 + q, 0:XS_W] = xs_ref[...]
    xp_ref[8:8 + q, XS_W:CONV_DIM] = bc_ref[...]
    for cc in range(CONV_DIM // CONV_COL_CHUNK):
        sl = slice(cc * CONV_COL_CHUNK, (cc + 1) * CONV_COL_CHUNK)
        conv = cb_ref[:, sl]
        for j in range(CONV_W):
            conv = conv + xp_ref[8 - (CONV_W - 1) + j:8 - (CONV_W - 1) + j + q, sl] * cw_ref[j:j + 1, sl]
        u_ref[:, sl] = conv * jax.nn.sigmoid(conv)

    @pl.when(c == pl.num_programs(1) - 1)
    def _():
        cn_ref[...] = xp_ref[8 + q - (CONV_W - 1):8 + q, :]

    xp_ref[0:8, :] = xp_ref[q:q + 8, :]

    lane = lax.broadcasted_iota(I32, (q, LANES), 1)
    head_lane = (lane >= SSD_HEAD_LANE) & (lane < SSD_HEAD_LANE + SSM_HEADS)
    pre = sm_ref[...] + dtb_ref[...]
    dt = jnp.where(head_lane, jnp.maximum(pre, 0.0) + jnp.log1p(jnp.exp(-jnp.abs(pre))), 0.0)
    ad = dt * (-jnp.exp(alog_ref[...]))
    row = lax.broadcasted_iota(I32, (q, q), 0)
    col = lax.broadcasted_iota(I32, (q, q), 1)
    tril = row >= col
    tri = jnp.where(tril, 1.0, 0.0).astype(BF16)
    acs = sum(jnp.dot(tri, part, preferred_element_type=F32) for part in _split3(ad))
    acs_last = acs[q - 1:q, :]
    expand = exp_ref[...]
    e1 = jnp.dot(jnp.exp(acs).astype(BF16), expand, preferred_element_type=F32)
    e2 = jnp.dot((jnp.exp(acs_last - acs) * dt).astype(BF16), expand, preferred_element_type=F32)
    dec8 = jnp.broadcast_to(jnp.exp(acs_last), (8, LANES))
    dec = sum(jnp.dot(part, expand, preferred_element_type=F32) for part in _split3(dec8))[0:1, :]
    acs_t = acs.T
    dt_t = dt.T
    lane_lo = lane < SSM_HEAD_DIM

    for g in range(SSM_GROUPS):
        gs = slice(g * GROUP_W, (g + 1) * GROUP_W)
        bg = u_ref[:, XS_W + g * D_STATE:XS_W + (g + 1) * D_STATE]
        cg = u_ref[:, XS_W + BC_W // 2 + g * D_STATE:XS_W + BC_W // 2 + (g + 1) * D_STATE].astype(BF16)
        cb = lax.dot_general(cg, bg.astype(BF16), NT_DIMS, preferred_element_type=F32)
        xs_g = u_ref[:, gs]
        y_diag = []
        for jp in range(SSM_HPG // 2):
            ms = []
            for r in (g * SSM_HPG + 2 * jp, g * SSM_HPG + 2 * jp + 1):
                hl = SSD_HEAD_LANE + r
                seg = acs[:, hl:hl + 1] - acs_t[hl:hl + 1, :]
                lmat = jnp.where(tril, jnp.exp(seg), 0.0)
                ms.append((cb * lmat * dt_t[hl:hl + 1, :]).astype(BF16))
            blk = xs_g[:, jp * LANES:(jp + 1) * LANES]
            rhs = jnp.concatenate([jnp.where(lane_lo, blk, 0.0), jnp.where(lane_lo, 0.0, blk)], axis=0)
            y_diag.append(jnp.dot(jnp.concatenate(ms, axis=1), rhs.astype(BF16), preferred_element_type=F32))
        h_old = ht_ref[:, gs]
        y_off = jnp.dot(cg, h_old.astype(BF16), preferred_element_type=F32) * e1[:, gs]
        y = jnp.concatenate(y_diag, axis=1) + y_off + dskip_ref[:, gs] * xs_g
        zz = z_ref[:, gs]
        y = y * (zz * jax.nn.sigmoid(zz))
        ms_y = jnp.sum(y * y, axis=-1, keepdims=True) * np.float32(1.0 / GROUP_W)
        y_ref[:, gs] = (y * lax.rsqrt(ms_y + EPS) * ng_ref[:, gs]).astype(BF16)
        upd = jnp.dot(bg.T.astype(BF16), (xs_g * e2[:, gs]).astype(BF16), preferred_element_type=F32)
        ht_ref[:, gs] = h_old * dec[:, gs] + upd

    @pl.when(c == pl.num_programs(1) - 1)
    def _():
        for j in range(D_INNER // LANES):
            h_ref[j * LANES:(j + 1) * LANES, :] = ht_ref[:, j * LANES:(j + 1) * LANES].T


def ssd_prompt(proj, conv_w, conv_b, dt_bias, a_log, d_skip, norm_g, bsz, seq):
    nc = seq // SSD_CHUNK
    n = bsz * seq
    rmap = lambda blk: (lambda b, c: (b * nc + c, blk))
    const = lambda a: pl.BlockSpec(a.shape, lambda b, c: (0, 0))
    consts = _ssd_consts(conv_w, conv_b, dt_bias, a_log, d_skip, norm_g)
    return pl.pallas_call(
        _ssd_kernel,
        out_shape=(jax.ShapeDtypeStruct((n, D_INNER), BF16),
                   jax.ShapeDtypeStruct((bsz, D_INNER, D_STATE), F32),
                   jax.ShapeDtypeStruct((bsz, CONV_W - 1, CONV_DIM), F32)),
        grid=(bsz, nc),
        in_specs=[pl.BlockSpec((SSD_CHUNK, D_INNER), rmap(OFF_Z // D_INNER)),
                  pl.BlockSpec((SSD_CHUNK, XS_W), rmap(OFF_XBC // XS_W)),
                  pl.BlockSpec((SSD_CHUNK, BC_W), rmap((OFF_XBC + XS_W) // BC_W)),
                  pl.BlockSpec((SSD_CHUNK, LANES), rmap(OFF_SMALL // LANES))] + [const(a) for a in consts],
        out_specs=(pl.BlockSpec((SSD_CHUNK, D_INNER), rmap(0)),
                   pl.BlockSpec((None, D_INNER, D_STATE), lambda b, c: (b, 0, 0)),
                   pl.BlockSpec((None, CONV_W - 1, CONV_DIM), lambda b, c: (b, 0, 0))),
        scratch_shapes=[pltpu.VMEM((SSD_CHUNK + 8, CONV_DIM), F32),
                        pltpu.VMEM((SSD_CHUNK, CONV_DIM), F32),
                        pltpu.VMEM((D_STATE, D_INNER), F32)],
        compiler_params=pltpu.CompilerParams(dimension_semantics=("parallel", "arbitrary"),
                                             vmem_limit_bytes=VMEM_LIMIT),
        name="ssd_prompt",
    )(proj, proj, proj, proj, *consts)


def _mix_kernel(ya_ref, yb_ref, ga_ref, gb_ref, x_ref, wa_ref, wb_ref, wo_ref, o_ref):
    ma = jnp.dot(ya_ref[...].astype(BF16), wa_ref[...], preferred_element_type=F32)
    mb = jnp.dot(yb_ref[...].astype(BF16), wb_ref[...], preferred_element_type=F32)
    mixed = jax.nn.sigmoid(ga_ref[...]) * ma + jax.nn.sigmoid(gb_ref[...]) * mb
    o_ref[...] = x_ref[...] + jnp.dot(mixed.astype(BF16), wo_ref[...], preferred_element_type=F32)


def branch_mix(ya, yb, proj, x, wa, wb, wo):
    n = x.shape[0]
    tm = _pick_tile(n, (256, 128))
    rows = lambda w, blk=0: pl.BlockSpec((tm, w), lambda i: (i, blk))
    whole = lambda a: pl.BlockSpec(a.shape, lambda i: (0, 0))
    return pl.pallas_call(
        _mix_kernel,
        out_shape=jax.ShapeDtypeStruct((n, D_MODEL), F32),
        grid=(n // tm,),
        in_specs=[rows(D_INNER), rows(ATT_WIDTH), rows(D_MODEL, OFF_GA // D_MODEL),
                  rows(D_MODEL, OFF_GB // D_MODEL), rows(D_MODEL), whole(wa), whole(wb), whole(wo)],
        out_specs=rows(D_MODEL),
        compiler_params=pltpu.CompilerParams(dimension_semantics=("parallel",),
                                             vmem_limit_bytes=VMEM_LIMIT),
        name="branch_mix",
    )(ya, yb, proj, proj, x, wa, wb, wo)


def _dsa_prompt_body(s_len, qb, iq_ref, sm_ref, ik_ref, q_ref, k_ref, v_ref, o_ref, jc_ref):
    ik = ik_ref[0:s_len, :].astype(BF16)
    iw = sm_ref[...].T[SMALL_IW:SMALL_IW + IDX_HEADS, :] * np.float32(IDX_SCALE)
    score = jnp.zeros((s_len, Q_BLOCK), F32)
    for h in range(IDX_HEADS):
        iqh = iq_ref[:, h * IDX_DIM:(h + 1) * IDX_DIM].astype(BF16)
        d = lax.dot_general(ik, iqh, NT_DIMS, preferred_element_type=F32)
        score = score + jnp.maximum(d, 0.0) * iw[h:h + 1, :]

    tpos = qb * Q_BLOCK + lax.broadcasted_iota(I32, (1, Q_BLOCK), 1)
    spos = lax.broadcasted_iota(I32, (s_len, 1), 0)
    causal = spos <= tpos
    bits = lax.bitcast_convert_type(score, I32)
    key = bits ^ ((bits >> 31) & np.int32(0x7FFFFFFF))
    key = jnp.where(score == 0.0, 0, key)
    key = jnp.where(causal, key, INT_MIN)

    def count(mask):
        return _tree_sum(mask.astype(I32), axis=0)

    def bisect_value(i, tu):
        cand_u = tu | jnp.left_shift(np.int32(1), 31 - i)
        return jnp.where(count(key >= (cand_u ^ INT_MIN)) >= TOPK_MAX, cand_u, tu)

    thr = lax.fori_loop(0, 32, bisect_value, jnp.zeros((1, Q_BLOCK), I32)) ^ INT_MIN
    above = key > thr
    tied = key == thr
    need = TOPK_MAX - count(above)
    excess = (count(tied) > need) & (thr != INT_MIN)

    jc_ref[...] = jnp.full(jc_ref.shape, s_len, I32)

    @pl.when(jnp.max(excess.astype(I32)) > 0)
    def _():
        def bisect_pos(i, j):
            cj = j | jnp.left_shift(np.int32(1), 10 - i)
            return jnp.where(count(tied & (spos < cj)) < need, cj, j)

        jc = lax.fori_loop(0, 11, bisect_pos, jnp.zeros((1, Q_BLOCK), I32))
        jc_ref[...] = jnp.broadcast_to(jc, jc_ref.shape)

    jc = jc_ref[0:1, :]
    sel = causal & (above | (tied & (spos <= jc)))
    bias_t = jnp.where(sel, 0.0, -jnp.inf)
    bias = jnp.concatenate([bias_t[i * Q_BLOCK:(i + 1) * Q_BLOCK, :].T for i in range(s_len // Q_BLOCK)], axis=1)

    for g in range(KV_HEADS):
        qs = jnp.concatenate(
            [q_ref[:, (ATT_REP * g + r) * HEAD_DIM:(ATT_REP * g + r + 1) * HEAD_DIM] for r in range(ATT_REP)],
            axis=0).astype(BF16)
        kg = k_ref[0:s_len, g * HEAD_DIM:(g + 1) * HEAD_DIM].astype(BF16)
        vg = v_ref[0:s_len, g * HEAD_DIM:(g + 1) * HEAD_DIM].astype(BF16)
        s = lax.dot_general(qs, kg, NT_DIMS, preferred_element_type=F32) * np.float32(HEAD_DIM ** -0.5)
        s = (s.reshape(ATT_REP, Q_BLOCK, s_len) + bias[None]).reshape(ATT_REP * Q_BLOCK, s_len)
        m = jnp.max(s, axis=-1, keepdims=True)
        p = jnp.exp(s - m)
        l = jnp.sum(p, axis=-1, keepdims=True)
        o = jnp.dot(p.astype(BF16), vg, preferred_element_type=F32) / l
        for r in range(ATT_REP):
            hh = ATT_REP * g + r
            o_ref[:, hh * HEAD_DIM:(hh + 1) * HEAD_DIM] = o[r * Q_BLOCK:(r + 1) * Q_BLOCK]


DSA_KEY_BUCKET = 512


def _dsa_prompt_kernel(*refs):
    qb = pl.program_id(1)
    s_full = refs[2].shape[0]
    per = DSA_KEY_BUCKET // Q_BLOCK
    for j in range(s_full // DSA_KEY_BUCKET):
        pl.when((qb >= per * j) & (qb < per * (j + 1)))(
            functools.partial(_dsa_prompt_body, DSA_KEY_BUCKET * (j + 1), qb, *refs))


def dsa_prompt(q, k, v, iq, ik, proj, bsz, seq):
    nq = seq // Q_BLOCK
    assert seq == 2048 and TOPK_MAX <= seq // 4
    qmap = lambda b, i: (b * nq + i, 0)
    bmap = lambda b, i: (b, 0)
    return pl.pallas_call(
        _dsa_prompt_kernel,
        out_shape=jax.ShapeDtypeStruct((bsz * seq, ATT_WIDTH), F32),
        grid=(bsz, nq),
        in_specs=[pl.BlockSpec((Q_BLOCK, IDX_HEADS * IDX_DIM), qmap),
                  pl.BlockSpec((Q_BLOCK, LANES), lambda b, i: (b * nq + i, OFF_SMALL // LANES)),
                  pl.BlockSpec((seq, IDX_DIM), bmap),
                  pl.BlockSpec((Q_BLOCK, ATT_WIDTH), qmap),
                  pl.BlockSpec((seq, KV_WIDTH), bmap),
                  pl.BlockSpec((seq, KV_WIDTH), bmap)],
        out_specs=pl.BlockSpec((Q_BLOCK, ATT_WIDTH), qmap),
        scratch_shapes=[pltpu.VMEM((8, LANES), I32)],
        compiler_params=pltpu.CompilerParams(dimension_semantics=("parallel", "arbitrary"),
                                             vmem_limit_bytes=VMEM_LIMIT),
        name="dsa_prompt",
    )(iq, proj, ik, q, k, v)


def _column_tile(row):
    d = row.shape[1]
    eye = lax.broadcasted_iota(I32, (d, d), 0) == lax.broadcasted_iota(I32, (d, d), 1)
    col = jnp.sum(jnp.where(eye, jnp.broadcast_to(row, (d, d)), 0.0), axis=1, keepdims=True)
    return jnp.where(lax.broadcasted_iota(I32, (d, LANES), 1) == 0, col, 0.0)


def _dsa_sample_kernel(layer, n_pages, pt_ref, iq_ref, iw_ref, ikn_ref, q_ref, kn_ref, vn_ref,
                       cik_hbm, ck_hbm, cv_hbm, o_ref, ikb, kb, vb, sems):
    b = pl.program_id(0)
    past = n_pages * PAGE_SIZE
    width = past + LANES

    streams = ((cik_hbm, ikb), (ck_hbm, kb), (cv_hbm, vb))

    def page_copy(j, which):
        src, dst = streams[which]
        cols = pl.ds(pl.multiple_of(j * PAGE_SIZE, PAGE_SIZE), PAGE_SIZE)
        return pltpu.make_async_copy(src.at[layer, pt_ref[b, j]], dst.at[..., cols], sems.at[which])

    def start_page(j, carry):
        for which in range(len(streams)):
            page_copy(j, which).start()
        return carry

    lax.fori_loop(0, n_pages, start_page, 0)

    ikb[:, past:width] = _column_tile(ikn_ref[...])
    for g in range(KV_HEADS):
        kb[g, :, past:width] = _column_tile(kn_ref[:, g * HEAD_DIM:(g + 1) * HEAD_DIM])
        vb[g, :, past:width] = _column_tile(vn_ref[:, g * HEAD_DIM:(g + 1) * HEAD_DIM])

    def wait_stream(which):
        def wait_page(j, carry):
            page_copy(j, which).wait()
            return carry
        lax.fori_loop(0, n_pages, wait_page, 0)

    wait_stream(0)
    iq16 = jnp.concatenate([iq_ref[...], jnp.zeros((BF16_ROWS - IDX_HEADS, IDX_DIM), F32)], axis=0)
    sc = jnp.dot(iq16.astype(BF16), ikb[...].astype(BF16), preferred_element_type=F32)[0:IDX_HEADS]
    score = jnp.sum(jnp.maximum(sc, 0.0) * (iw_ref[...] * np.float32(IDX_SCALE)), axis=0, keepdims=True)
    fw = -(-width // (8 * LANES)) * LANES
    score = jnp.concatenate([score, jnp.zeros((1, 8 * fw - width), F32)], axis=1)
    score = jnp.concatenate([score[:, r * fw:(r + 1) * fw] for r in range(8)], axis=0)
    pos = lax.broadcasted_iota(I32, (8, fw), 0) * fw + lax.broadcasted_iota(I32, (8, fw), 1)
    allowed = pos <= past
    bits = lax.bitcast_convert_type(score, I32)
    key = bits ^ ((bits >> 31) & np.int32(0x7FFFFFFF))
    key = jnp.where(score == 0.0, 0, key)
    key = jnp.where(allowed, key, INT_MIN)

    def count(mask):
        return jnp.sum(_tree_sum(mask.astype(I32), axis=1), axis=0, keepdims=True)

    def bisect_value(i, tu):
        cand_u = tu | jnp.left_shift(np.int32(1), 31 - i)
        return jnp.where(count(key >= (cand_u ^ INT_MIN)) >= TOPK_MAX, cand_u, tu)

    thr = lax.fori_loop(0, 32, bisect_value, jnp.zeros((1, 1), I32)) ^ INT_MIN
    above = key > thr
    tied = key == thr
    need = TOPK_MAX - count(above)
    pos_bits = int(8 * fw - 1).bit_length()

    def bisect_pos(i, j):
        cj = j | jnp.left_shift(np.int32(1), pos_bits - 1 - i)
        return jnp.where(count(tied & (pos < cj)) < need, cj, j)

    jc = lax.fori_loop(0, pos_bits, bisect_pos, jnp.zeros((1, 1), I32))
    sel = allowed & (above | (tied & (pos <= jc)))
    bias = jnp.where(sel, 0.0, -jnp.inf)
    bias = jnp.concatenate([bias[r:r + 1, :] for r in range(8)], axis=1)[:, 0:width]

    wait_stream(1)
    wait_stream(2)
    for g in range(KV_HEADS):
        qg = jnp.concatenate(
            [q_ref[:, (ATT_REP * g + r) * HEAD_DIM:(ATT_REP * g + r + 1) * HEAD_DIM] for r in range(ATT_REP)]
            + [jnp.zeros((BF16_ROWS - ATT_REP, HEAD_DIM), F32)], axis=0).astype(BF16)
        s = jnp.dot(qg, kb[g].astype(BF16), preferred_element_type=F32) * np.float32(HEAD_DIM ** -0.5) + bias
        m = jnp.max(s, axis=-1, keepdims=True)
        p = jnp.exp(s - m)
        l = jnp.sum(p, axis=-1, keepdims=True)
        o = lax.dot_general(p.astype(BF16), vb[g].astype(BF16), NT_DIMS, preferred_element_type=F32) / l
        for r in range(ATT_REP):
            hh = ATT_REP * g + r
            o_ref[:, hh * HEAD_DIM:(hh + 1) * HEAD_DIM] = o[r:r + 1]


def dsa_sample(layer, q, k, v, iq, ik, iw, ck_t, cv_t, cik_t, page_table):
    bs, n_pages = page_table.shape
    width = n_pages * PAGE_SIZE + LANES
    assert TOPK_MAX <= (n_pages * PAGE_SIZE + 1) // 4
    row = lambda w: pl.BlockSpec((None, 1, w), lambda b, pt: (b, 0, 0))
    anyspec = pl.BlockSpec(memory_space=pl.ANY)
    grid_spec = pltpu.PrefetchScalarGridSpec(
        num_scalar_prefetch=1,
        grid=(bs,),
        in_specs=[pl.BlockSpec((None, IDX_HEADS, IDX_DIM), lambda b, pt: (b, 0, 0)),
                  pl.BlockSpec((None, IDX_HEADS, 1), lambda b, pt: (b, 0, 0)),
                  row(IDX_DIM), row(ATT_WIDTH), row(KV_WIDTH), row(KV_WIDTH),
                  anyspec, anyspec, anyspec],
        out_specs=row(ATT_WIDTH),
        scratch_shapes=[pltpu.VMEM((IDX_DIM, width), F32),
                        pltpu.VMEM((KV_HEADS, HEAD_DIM, width), F32),
                        pltpu.VMEM((KV_HEADS, HEAD_DIM, width), F32),
                        pltpu.SemaphoreType.DMA((3,))])
    out = pl.pallas_call(
        functools.partial(_dsa_sample_kernel, layer, n_pages),
        out_shape=jax.ShapeDtypeStruct((bs, 1, ATT_WIDTH), F32),
        grid_spec=grid_spec,
        compiler_params=pltpu.CompilerParams(dimension_semantics=("arbitrary",),
                                             vmem_limit_bytes=VMEM_LIMIT),
        name="dsa_sample",
    )(page_table, iq.reshape(bs, IDX_HEADS, IDX_DIM), iw.reshape(bs, IDX_HEADS, 1),
      ik.reshape(bs, 1, IDX_DIM), q.reshape(bs, 1, ATT_WIDTH), k.reshape(bs, 1, KV_WIDTH),
      v.reshape(bs, 1, KV_WIDTH), cik_t, ck_t, cv_t)
    return out.reshape(bs, ATT_WIDTH)


def _row_to_cols(row):
    nblk = row.shape[1] // LANES
    stacked = jnp.concatenate([row[:, j * LANES:(j + 1) * LANES] for j in range(nblk)]
                              + [jnp.zeros((LANES - nblk, LANES), F32)], axis=0)
    return stacked.T


def _ssd_step_kernel(z_ref, xs_ref, bc_ref, sm_ref, cp_ref, h_ref, cw_ref, cb_ref, dtb_ref, alog_ref,
                     dskip_ref, ng_ref, exp_ref, y_ref, ho_ref, cn_ref):
    xbc = jnp.concatenate([xs_ref[...], bc_ref[...]], axis=1)
    conv = cb_ref[...] + xbc * cw_ref[CONV_W - 1:CONV_W, :]
    for j in range(CONV_W - 1):
        conv = conv + cp_ref[j:j + 1, :] * cw_ref[j:j + 1, :]
    cn_ref[0:CONV_W - 2, :] = cp_ref[1:CONV_W - 1, :]
    cn_ref[CONV_W - 2:CONV_W - 1, :] = xbc
    u = conv * jax.nn.sigmoid(conv)
    xs = u[:, 0:XS_W]

    lane = lax.broadcasted_iota(I32, (1, LANES), 1)
    head_lane = (lane >= SSD_HEAD_LANE) & (lane < SSD_HEAD_LANE + SSM_HEADS)
    pre = sm_ref[...] + dtb_ref[...]
    dt = jnp.where(head_lane, jnp.maximum(pre, 0.0) + jnp.log1p(jnp.exp(-jnp.abs(pre))), 0.0)
    decay = jnp.exp(dt * (-jnp.exp(alog_ref[...])))
    expand = exp_ref[...]
    widen = lambda r: sum(jnp.dot(part, expand, preferred_element_type=F32)
                          for part in _split3(jnp.broadcast_to(r, (8, LANES))))[0:1, :]
    xd_cols = _row_to_cols(xs * widen(dt))
    dec_cols = _row_to_cols(widen(decay))

    y_cols = jnp.zeros((LANES, LANES), F32)
    col_id = lax.broadcasted_iota(I32, (LANES, LANES), 1)
    for j in range(D_INNER // LANES):
        g = j * LANES // GROUP_W
        b_row = u[:, XS_W + g * D_STATE:XS_W + (g + 1) * D_STATE]
        c_row = u[:, XS_W + BC_W // 2 + g * D_STATE:XS_W + BC_W // 2 + (g + 1) * D_STATE]
        rows = slice(j * LANES, (j + 1) * LANES)
        h_new = h_ref[rows, :] * dec_cols[:, j:j + 1] + xd_cols[:, j:j + 1] * b_row
        ho_ref[rows, :] = h_new
        y_cols = jnp.where(col_id == j, jnp.sum(h_new * c_row, axis=1, keepdims=True), y_cols)
    y_rows = y_cols.T
    y = jnp.concatenate([y_rows[j:j + 1, :] for j in range(D_INNER // LANES)], axis=1)
    y = y + dskip_ref[...] * xs
    zz = z_ref[...]
    y = y * (zz * jax.nn.sigmoid(zz))
    for g in range(SSM_GROUPS):
        gs = slice(g * GROUP_W, (g + 1) * GROUP_W)
        yg = y[:, gs]
        ms = jnp.sum(yg * yg, axis=-1, keepdims=True) * np.float32(1.0 / GROUP_W)
        y_ref[:, gs] = (yg * lax.rsqrt(ms + EPS) * ng_ref[:, gs]).astype(BF16)


def _ssd_consts(conv_w, conv_b, dt_bias, a_log, d_skip, norm_g):
    head_row = lambda v: jnp.zeros((1, LANES), F32).at[0, SSD_HEAD_LANE:SSD_HEAD_LANE + SSM_HEADS].set(v)
    expand = (jnp.arange(LANES)[:, None] == SSD_HEAD_LANE + jnp.arange(D_INNER)[None, :] // SSM_HEAD_DIM).astype(BF16)
    return (conv_w, conv_b.reshape(1, CONV_DIM), head_row(dt_bias), head_row(a_log),
            jnp.repeat(d_skip, SSM_HEAD_DIM).reshape(1, D_INNER), norm_g.reshape(1, D_INNER), expand)


def ssd_step(proj, conv_state, ssm_state, layer, conv_w, conv_b, dt_bias, a_log, d_skip, norm_g):
    bs = proj.shape[0]
    proj3 = proj.reshape(bs, 1, IN_WIDTH_PAD)
    states = ssm_state.reshape(ssm_state.shape[0], bs, D_INNER, D_STATE)
    row = lambda w, off: pl.BlockSpec((None, 1, w), lambda b: (b, 0, off // w))
    const = lambda a: pl.BlockSpec(a.shape, lambda b: (0, 0))
    consts = _ssd_consts(conv_w, conv_b, dt_bias, a_log, d_skip, norm_g)
    y, h_new, conv_new = pl.pallas_call(
        _ssd_step_kernel,
        out_shape=(jax.ShapeDtypeStruct((bs, 1, D_INNER), BF16),
                   jax.ShapeDtypeStruct((bs, D_INNER, D_STATE), F32),
                   jax.ShapeDtypeStruct((bs, CONV_W - 1, CONV_DIM), F32)),
        grid=(bs,),
        in_specs=[row(D_INNER, OFF_Z), row(XS_W, OFF_XBC), row(BC_W, OFF_XBC + XS_W), row(LANES, OFF_SMALL),
                  pl.BlockSpec((None, CONV_W - 1, CONV_DIM), lambda b: (b, 0, 0)),
                  pl.BlockSpec((None, None, D_INNER, D_STATE), lambda b: (layer, b, 0, 0))]
                 + [const(a) for a in consts],
        out_specs=(pl.BlockSpec((None, 1, D_INNER), lambda b: (b, 0, 0)),
                   pl.BlockSpec((None, D_INNER, D_STATE), lambda b: (b, 0, 0)),
                   pl.BlockSpec((None, CONV_W - 1, CONV_DIM), lambda b: (b, 0, 0))),
        compiler_params=pltpu.CompilerParams(dimension_semantics=("parallel",),
                                             vmem_limit_bytes=VMEM_LIMIT),
        name="ssd_step",
    )(proj3, proj3, proj3, proj3, conv_state, states, *consts)
    return y.reshape(bs, D_INNER), h_new, conv_new


def _layer(x, pos, lw, conv_prefix, h0, paged):
    (norm1_g, w_in_bf, conv_w, conv_b, dt_bias, a_log, d_skip, ssm_norm_g, q_norm_g, k_norm_g,
     w_a_bf, w_b_bf, w_out_bf, norm2_g, wq_bf, keys_bf, ut_bf, v_bf) = lw
    bsz, t_len, _ = x.shape
    n = bsz * t_len
    x2 = x.reshape(n, D_MODEL)
    proj, _ = norm_matmul(x2, norm1_g, w_in_bf)
    cos_t, sin_t = rope_tables(pos)
    q, k, v, iq, ik = attn_prep(proj, cos_t, sin_t, q_norm_g, k_norm_g)
    ssd_params = (conv_w, conv_b, dt_bias, a_log, d_skip, ssm_norm_g)
    if paged is None:
        y_a, h_new, conv_new = ssd_prompt(proj, *ssd_params, bsz, t_len)
    else:
        assert t_len == 1
        y_a, h_new, conv_new = ssd_step(proj, conv_prefix, h0, paged[0], *ssd_params)
    k4 = k.reshape(bsz, t_len, KV_HEADS, HEAD_DIM)
    v4 = v.reshape(bsz, t_len, KV_HEADS, HEAD_DIM)
    ik3 = ik.reshape(bsz, t_len, IDX_DIM)
    if paged is None:
        y_b = dsa_prompt(q, k, v, iq, ik, proj, bsz, t_len)
    else:
        y_b = dsa_sample(paged[0], q, k, v, iq, ik, proj[:, OFF_SMALL + SMALL_IW:OFF_SMALL + SMALL_IW + IDX_HEADS],
                         *paged[1:])
    x2 = branch_mix(y_a.reshape(n, D_INNER), y_b, proj, x2, w_a_bf, w_b_bf, w_out_bf)
    x2 = peer_block(x2, norm2_g, wq_bf, keys_bf, ut_bf, v_bf)
    h_out = h_new.reshape(bsz, SSM_HEADS, SSM_HEAD_DIM, D_STATE)
    return x2.reshape(bsz, t_len, D_MODEL), k4, v4, ik3, h_out, conv_new


def kernel(x_prompt, x_sample, cache_k, cache_v, cache_idx_k, state_ssm, state_conv, page_table,
           norm1_g, w_in, conv_w, conv_b, dt_bias, a_log, d_skip, ssm_norm_g, q_norm_g, k_norm_g,
           w_branch_a, w_branch_b, w_out, norm2_g, peer_wq, peer_keys, peer_u, peer_v):
    bp, seq = x_prompt.shape[:2]
    bs, t_new = x_sample.shape[:2]
    past = page_table.shape[1] * PAGE_SIZE
    pos_p = jnp.arange(seq)
    pos_s = jnp.tile(past + jnp.arange(t_new), bs)
    yp, ys = x_prompt, x_sample
    outs_p, outs_s = [], []
    ck_t = jnp.transpose(cache_k, (0, 1, 3, 4, 2))
    cv_t = jnp.transpose(cache_v, (0, 1, 3, 4, 2))
    cik_t = jnp.transpose(cache_idx_k, (0, 1, 3, 2))
    sizes = dict(zip(("z", "xbc", "dt", "q", "k", "v", "iq", "ik", "iw", "ga", "gb"), IN_SIZES))
    starts = dict(zip(sizes, np.cumsum((0,) + IN_SIZES[:-1])))
    order = ("z", "xbc", "q", "k", "v", "iq", "ga", "gb", "ik", "dt", "iw")
    for l in range(DEPTH):
        w_bf = w_in[l].astype(BF16)
        cols = [w_bf[:, int(starts[s]):int(starts[s]) + sizes[s]] for s in order]
        used = sum(sizes.values())
        w_in_pad = jnp.concatenate(cols + [jnp.zeros((D_MODEL, IN_WIDTH_PAD - used), BF16)], axis=1)
        lw = (norm1_g[l], w_in_pad, conv_w[l], conv_b[l], dt_bias[l], a_log[l], d_skip[l],
              ssm_norm_g[l], q_norm_g[l], k_norm_g[l], w_branch_a[l].astype(BF16),
              w_branch_b[l].astype(BF16), w_out[l].astype(BF16), norm2_g[l],
              peer_wq[l].astype(BF16),
              peer_keys[l].astype(BF16).reshape(2 * PEER_HEADS, N_KEYS, PEER_HALF),
              peer_u[l].T.astype(BF16), peer_v[l].astype(BF16))
        yp, *rest = _layer(yp, pos_p, lw, None, None, None)
        outs_p.append(rest)
        ys, *rest = _layer(ys, pos_s, lw, state_conv[l], state_ssm, (l, ck_t, cv_t, cik_t, page_table))
        outs_s.append(rest)
    stack = lambda outs, i: jnp.stack([o[i] for o in outs])
    return (yp, ys,
            stack(outs_p, 0), stack(outs_p, 1), stack(outs_p, 2), stack(outs_p, 3), stack(outs_p, 4),
            stack(outs_s, 0), stack(outs_s, 1), stack(outs_s, 2), stack(outs_s, 3), stack(outs_s, 4))
```

```python
import functools
import math

import jax
import jax.numpy as jnp
import numpy as np
from jax import lax
from jax.experimental import pallas as pl
from jax.experimental.pallas import tpu as pltpu

D_MODEL = 1024
DEPTH = 2
PAGE_SIZE = 128
D_INNER = 2048
SSM_HEAD_DIM = 64
SSM_HEADS = 32
SSM_GROUPS = 4
SSM_HPG = 8
D_STATE = 128
CONV_W = 4
CONV_DIM = D_INNER + 2 * SSM_GROUPS * D_STATE
SSD_CHUNK = 128
ATT_HEADS = 16
KV_HEADS = 4
HEAD_DIM = 64
ATT_REP = 4
ATT_WIDTH = 1024
KV_WIDTH = 256
IDX_HEADS = 8
IDX_DIM = 64
IDX_SCALE = (IDX_HEADS * IDX_DIM) ** -0.5
TOPK_MAX = 256
Q_BLOCK = 128
ROPE_THETA = 10000.0
PEER_HEADS = 8
N_KEYS = 128
N_EXPERTS = N_KEYS * N_KEYS
PEER_KEY_DIM = 256
PEER_HALF = 128
PEER_TOPK = 16
PEER_PAIRS = PEER_HEADS * PEER_TOPK
EPS = 1e-6
IN_SIZES = (D_INNER, CONV_DIM, SSM_HEADS, ATT_WIDTH, KV_WIDTH, KV_WIDTH,
            IDX_HEADS * IDX_DIM, IDX_DIM, IDX_HEADS, D_MODEL, D_MODEL)
IN_WIDTH = sum(IN_SIZES)
OFF_Z, OFF_XBC, OFF_Q, OFF_KV, OFF_IQ, OFF_GA, OFF_GB, OFF_SMALL = 0, 2048, 5120, 6144, 6656, 7168, 8192, 9216
SMALL_IK, SMALL_DT, SMALL_IW = 0, 64, 96
IN_WIDTH_PAD = 9728

F32 = jnp.float32
BF16 = jnp.bfloat16
I32 = jnp.int32
INT_MIN = np.int32(-2 ** 31)
VMEM_LIMIT = 56 * 1024 * 1024
LANES = 128
BF16_ROWS = 16
NT_DIMS = (((1,), (1,)), ((), ()))


def _tree_sum(x, axis):
    step = 8 if axis == 0 else LANES
    size = x.shape[axis]
    take = (lambda i: x[i:i + step, :]) if axis == 0 else (lambda i: x[:, i:i + step])
    parts = [take(i) for i in range(0, size, step)]
    while len(parts) > 1:
        pairs = [parts[i] + parts[i + 1] for i in range(0, len(parts) - 1, 2)]
        parts = pairs + ([parts[-1]] if len(parts) % 2 else [])
    return jnp.sum(parts[0], axis=axis, keepdims=True)


def _pick_tile(n, candidates):
    for c in candidates:
        if n % c == 0:
            return c
    return n


def _mm_norm_kernel(x_ref, g_ref, w_ref, o_ref, xn_ref):
    @pl.when(pl.program_id(1) == 0)
    def _():
        x = x_ref[...]
        ms = jnp.mean(x * x, axis=-1, keepdims=True)
        xn_ref[...] = (x * lax.rsqrt(ms + EPS) * g_ref[...]).astype(BF16)

    o_ref[...] = jnp.dot(xn_ref[...], w_ref[...], preferred_element_type=F32)


def norm_matmul(x, gain, w):
    m, k = x.shape
    n = w.shape[1]
    tm = _pick_tile(m, (2048, 1024, 512, 256, 128))
    tn = _pick_tile(n, (512, 256, 128))
    return pl.pallas_call(
        _mm_norm_kernel,
        out_shape=(jax.ShapeDtypeStruct((m, n), F32), jax.ShapeDtypeStruct((m, k), BF16)),
        grid=(m // tm, n // tn),
        in_specs=[pl.BlockSpec((tm, k), lambda i, j: (i, 0)),
                  pl.BlockSpec((1, k), lambda i, j: (0, 0)),
                  pl.BlockSpec((k, tn), lambda i, j: (0, j))],
        out_specs=(pl.BlockSpec((tm, tn), lambda i, j: (i, j)),
                   pl.BlockSpec((tm, k), lambda i, j: (i, 0))),
        compiler_params=pltpu.CompilerParams(dimension_semantics=("parallel", "arbitrary"),
                                             vmem_limit_bytes=VMEM_LIMIT),
        name="norm_matmul",
    )(x, gain.reshape(1, k), w)


def _top16_rows(s):
    r = s.shape[0]
    rows = lax.broadcasted_iota(I32, s.shape, 0)
    vals, idxs = [], []
    for _ in range(PEER_TOPK):
        m = jnp.max(s, axis=0, keepdims=True)
        i = jnp.min(jnp.where(s == m, rows, r), axis=0, keepdims=True)
        vals.append(m)
        idxs.append(i)
        s = jnp.where(rows == i, -jnp.inf, s)
    return jnp.concatenate(vals, axis=0), jnp.concatenate(idxs, axis=0)


def _peer_pairs(sv0, si0, sv1, si1):
    t = sv0.shape[1]
    row8 = lax.broadcasted_iota(I32, (8, t), 0)
    row16 = lax.broadcasted_iota(I32, (16, t), 0)
    vals = [sv0[0:1] + sv1]
    eids = [si0[0:1] * N_KEYS + si1]
    flat = [row16]
    for a in range(1, 8):
        nb = PEER_TOPK // (a + 1)
        v = sv0[a:a + 1] + sv1[0:8]
        vals.append(v if nb >= 8 else jnp.where(row8 < nb, v, -jnp.inf))
        eids.append(si0[a:a + 1] * N_KEYS + si1[0:8])
        flat.append(row8 + PEER_TOPK * a)
    vals.append(sv0[8:16] + sv1[0:1])
    eids.append(si0[8:16] * N_KEYS + si1[0:1])
    flat.append((row8 + 8) * PEER_TOPK)
    cand = jnp.concatenate(vals, axis=0)
    ce = jnp.concatenate(eids, axis=0)
    cf = jnp.concatenate(flat, axis=0)
    fv, fe = [], []
    for _ in range(PEER_TOPK):
        m = jnp.max(cand, axis=0, keepdims=True)
        pick = jnp.min(jnp.where(cand == m, cf, PEER_TOPK * PEER_TOPK), axis=0, keepdims=True)
        sel = cf == pick
        fe.append(jnp.max(jnp.where(sel, ce, -1), axis=0, keepdims=True))
        fv.append(m)
        cand = jnp.where(sel, -jnp.inf, cand)
    fv = jnp.concatenate(fv, axis=0)
    fe = jnp.concatenate(fe, axis=0)
    p = jnp.exp(fv - fv[0:1])
    return fe, p / jnp.sum(p, axis=0, keepdims=True)


def _peer_topk_kernel(q_ref, keys_ref, a_ref, b_ref, g_ref):
    q = q_ref[...].astype(BF16)
    es, gs = [], []
    for h in range(PEER_HEADS):
        tops = []
        for c in range(2):
            hc = 2 * h + c
            s = lax.dot_general(keys_ref[hc], q[:, hc * PEER_HALF:(hc + 1) * PEER_HALF], NT_DIMS,
                                preferred_element_type=F32)
            tops.append(_top16_rows(s))
        e, g = _peer_pairs(tops[0][0], tops[0][1], tops[1][0], tops[1][1])
        es.append(e)
        gs.append(g)
    e = jnp.concatenate(es, axis=0)
    g = jnp.concatenate(gs, axis=0)
    a_ref[...] = (e >> 7).astype(F32).T
    b_ref[...] = (e & (N_KEYS - 1)).astype(F32).T
    g_ref[...] = g.T


PEER_TOPK_TILE = 128


def peer_topk(q, keys_bf):
    n = q.shape[0]
    tt = PEER_TOPK_TILE
    spec = pl.BlockSpec((tt, PEER_PAIRS), lambda i: (i, 0))
    shp = jax.ShapeDtypeStruct((n, PEER_PAIRS), F32)
    return pl.pallas_call(
        _peer_topk_kernel,
        out_shape=(shp, shp, shp),
        grid=(n // tt,),
        in_specs=[pl.BlockSpec((tt, PEER_HEADS * PEER_KEY_DIM), lambda i: (i, 0)),
                  pl.BlockSpec((2 * PEER_HEADS, N_KEYS, PEER_HALF), lambda i: (0, 0, 0))],
        out_specs=(spec, spec, spec),
        compiler_params=pltpu.CompilerParams(dimension_semantics=("parallel",),
                                             vmem_limit_bytes=VMEM_LIMIT),
        name="peer_topk",
    )(q, keys_bf)


W_PITCH = 136
PEER_W_TILE = 64


def _peer_w_kernel(a_ref, b_ref, g_ref, o_ref, wf_ref):
    tt = a_ref.shape[0]
    ii = lax.broadcasted_iota(I32, (N_KEYS, PEER_PAIRS), 0).astype(F32)

    def body(t, carry):
        a = a_ref[pl.ds(t, 1), :]
        b = b_ref[pl.ds(t, 1), :]
        g = g_ref[pl.ds(t, 1), :]
        at = jnp.where(ii == a, 1.0, 0.0).astype(BF16)
        bt = jnp.where(ii == b, g, 0.0).astype(BF16)
        w = lax.dot_general(at, bt, NT_DIMS, preferred_element_type=F32)
        wf_ref[pl.ds(pl.multiple_of(t * W_PITCH, 8), N_KEYS), :] = w
        return carry

    lax.fori_loop(0, tt, body, 0, unroll=16)
    for i1 in range(N_KEYS):
        o_ref[:, i1 * N_KEYS:(i1 + 1) * N_KEYS] = wf_ref[pl.ds(i1, tt, stride=W_PITCH), :].astype(BF16)


def peer_gate_matrix(a, b, g):
    n = a.shape[0]
    tt = PEER_W_TILE
    spec = pl.BlockSpec((tt, PEER_PAIRS), lambda i: (i, 0))
    return pl.pallas_call(
        _peer_w_kernel,
        out_shape=jax.ShapeDtypeStruct((n, N_EXPERTS), BF16),
        grid=(n // tt,),
        in_specs=[spec, spec, spec],
        out_specs=pl.BlockSpec((tt, N_EXPERTS), lambda i: (i, 0)),
        scratch_shapes=[pltpu.VMEM((tt * W_PITCH, N_KEYS), F32)],
        compiler_params=pltpu.CompilerParams(dimension_semantics=("parallel",),
                                             vmem_limit_bytes=VMEM_LIMIT),
        name="peer_gate_matrix",
    )(a, b, g)


def _peer_dense_kernel(xn_ref, x_ref, ut_ref, w_ref, v_ref, o_ref):
    @pl.when(pl.program_id(1) == 0)
    def _():
        o_ref[...] = x_ref[...]

    h = jnp.dot(xn_ref[...], ut_ref[...], preferred_element_type=F32)
    act = 0.5 * h * (1.0 + lax.erf(h * np.float32(math.sqrt(0.5))))
    wa = (w_ref[...].astype(F32) * act).astype(BF16)
    o_ref[...] += jnp.dot(wa, v_ref[...], preferred_element_type=F32)


PEER_EXPERT_TILE = 1024


def peer_dense(xn, x, ut, w, v):
    n = x.shape[0]
    tm = _pick_tile(n, (1024, 512, 256, 128))
    te = PEER_EXPERT_TILE
    return pl.pallas_call(
        _peer_dense_kernel,
        out_shape=jax.ShapeDtypeStruct((n, D_MODEL), F32),
        grid=(n // tm, N_EXPERTS // te),
        in_specs=[pl.BlockSpec((tm, D_MODEL), lambda i, j: (i, 0)),
                  pl.BlockSpec((tm, D_MODEL), lambda i, j: (i, 0)),
                  pl.BlockSpec((D_MODEL, te), lambda i, j: (0, j)),
                  pl.BlockSpec((tm, te), lambda i, j: (i, j)),
                  pl.BlockSpec((te, D_MODEL), lambda i, j: (j, 0))],
        out_specs=pl.BlockSpec((tm, D_MODEL), lambda i, j: (i, 0)),
        compiler_params=pltpu.CompilerParams(dimension_semantics=("parallel", "arbitrary"),
                                             vmem_limit_bytes=VMEM_LIMIT),
        name="peer_dense",
    )(xn, x, ut, w, v)


def peer_block(x2, norm2_g, wq_bf, keys_bf, ut_bf, v_bf):
    n = x2.shape[0]
    npad = -(-n // PEER_TOPK_TILE) * PEER_TOPK_TILE
    xp = jnp.pad(x2, ((0, npad - n), (0, 0))) if npad != n else x2
    q, xn = norm_matmul(xp, norm2_g, wq_bf)
    a, b, g = peer_topk(q, keys_bf)
    w = peer_gate_matrix(a, b, g)
    out = peer_dense(xn, xp, ut_bf, w, v_bf)
    return out[:n] if npad != n else out


def _group_sum64(xx, bd):
    hi = xx.astype(BF16)
    r1 = xx - hi.astype(F32)
    mid = r1.astype(BF16)
    lo = (r1 - mid.astype(F32)).astype(BF16)
    dot = lambda a: jnp.dot(a, bd, preferred_element_type=F32)
    return dot(hi) + dot(mid) + dot(lo)


def _rope128(y, c, s, lo_half):
    partner = jnp.where(lo_half, pltpu.roll(y, LANES - 32, axis=1), pltpu.roll(y, 32, axis=1))
    return y * c + partner * s


def _attn_prep_kernel(q_ref, kv_ref, iq_ref, sm_ref, cos_ref, sin_ref, qg_ref, kg_ref,
                      qo_ref, ko_ref, vo_ref, iqo_ref, iko_ref):
    tm = q_ref.shape[0]
    lane = lax.broadcasted_iota(I32, (tm, LANES), 1)
    lo_half = (lane & (HEAD_DIM - 1)) < HEAD_DIM // 2
    grp_r = lax.broadcasted_iota(I32, (LANES, LANES), 0) >> 6
    grp_c = lax.broadcasted_iota(I32, (LANES, LANES), 1) >> 6
    bd = jnp.where(grp_r == grp_c, 1.0, 0.0).astype(BF16)
    c = cos_ref[...]
    s = sin_ref[...]

    def normed(x, g):
        ms = _group_sum64(x * x, bd) * np.float32(1.0 / HEAD_DIM)
        return x * lax.rsqrt(ms + EPS) * g

    for j in range(ATT_WIDTH // LANES):
        sl = slice(j * LANES, (j + 1) * LANES)
        qo_ref[:, sl] = _rope128(normed(q_ref[:, sl], qg_ref[...]), c, s, lo_half) * np.float32(HEAD_DIM ** -0.5)
    for j in range(KV_WIDTH // LANES):
        sl = slice(j * LANES, (j + 1) * LANES)
        ko_ref[:, sl] = _rope128(normed(kv_ref[:, sl], kg_ref[...]), c, s, lo_half)
    vo_ref[...] = kv_ref[:, KV_WIDTH:2 * KV_WIDTH]
    for j in range(IDX_HEADS * IDX_DIM // LANES):
        sl = slice(j * LANES, (j + 1) * LANES)
        iqo_ref[:, sl] = _rope128(iq_ref[:, sl], c, s, lo_half)
    iko_ref[...] = _rope128(sm_ref[...], c, s, lo_half)[:, SMALL_IK:SMALL_IK + IDX_DIM]


def attn_prep(proj, cos_t, sin_t, q_norm_g, k_norm_g):
    n = proj.shape[0]
    tm = _pick_tile(n, (256, 128))
    nt = cos_t.shape[0] // tm
    row = lambda w, off: pl.BlockSpec((tm, w), lambda i: (i, off // w))
    tab = pl.BlockSpec((tm, LANES), lambda i: (i % nt, 0))
    gain = pl.BlockSpec((1, LANES), lambda i: (0, 0))
    out = lambda w: pl.BlockSpec((tm, w), lambda i: (i, 0))
    shp = lambda w: jax.ShapeDtypeStruct((n, w), F32)
    tile2 = lambda g: jnp.tile(g.reshape(1, HEAD_DIM), (1, LANES // HEAD_DIM))
    return pl.pallas_call(
        _attn_prep_kernel,
        out_shape=(shp(ATT_WIDTH), shp(KV_WIDTH), shp(KV_WIDTH), shp(IDX_HEADS * IDX_DIM), shp(IDX_DIM)),
        grid=(n // tm,),
        in_specs=[row(ATT_WIDTH, OFF_Q), row(2 * KV_WIDTH, OFF_KV), row(IDX_HEADS * IDX_DIM, OFF_IQ),
                  row(LANES, OFF_SMALL), tab, tab, gain, gain],
        out_specs=(out(ATT_WIDTH), out(KV_WIDTH), out(KV_WIDTH), out(IDX_HEADS * IDX_DIM), out(IDX_DIM)),
        compiler_params=pltpu.CompilerParams(dimension_semantics=("parallel",),
                                             vmem_limit_bytes=VMEM_LIMIT),
        name="attn_prep",
    )(proj, proj, proj, proj, cos_t, sin_t, tile2(q_norm_g), tile2(k_norm_g))


def rope_tables(pos):
    inv = ROPE_THETA ** (-jnp.arange(0, HEAD_DIM, 2, dtype=F32) / HEAD_DIM)
    ang = pos.astype(F32)[:, None] * inv[None, :]
    cos, sin = jnp.cos(ang), jnp.sin(ang)
    reps = LANES // HEAD_DIM
    return (jnp.tile(jnp.concatenate([cos, cos], axis=-1), (1, reps)),
            jnp.tile(jnp.concatenate([-sin, sin], axis=-1), (1, reps)))


SSD_HEAD_LANE = SMALL_DT
XS_W = D_INNER
BC_W = 2 * SSM_GROUPS * D_STATE
GROUP_W = SSM_HPG * SSM_HEAD_DIM
CONV_COL_CHUNK = 512


def _split3(x):
    hi = x.astype(BF16)
    r1 = x - hi.astype(F32)
    mid = r1.astype(BF16)
    return hi, mid, (r1 - mid.astype(F32)).astype(BF16)


def _ssd_kernel(z_ref, xs_ref, bc_ref, sm_ref, cw_ref, cb_ref, dtb_ref, alog_ref, dskip_ref, ng_ref,
                exp_ref, y_ref, h_ref, cn_ref, xp_ref, u_ref, ht_ref):
    c = pl.program_id(1)
    q = SSD_CHUNK

    @pl.when(c == 0)
    def _():
        xp_ref[0:8, :] = jnp.zeros((8, CONV_DIM), F32)
        ht_ref[...] = jnp.zeros(ht_ref.shape, F32)

    xp_ref[8:8 + q, 0:XS_W] = xs_ref[...]
    xp_ref[8:8 + q, XS_W:CONV_DIM] = bc_ref[...]
    for cc in range(CONV_DIM // CONV_COL_CHUNK):
        sl = slice(cc * CONV_COL_CHUNK, (cc + 1) * CONV_COL_CHUNK)
        conv = cb_ref[:, sl]
        for j in range(CONV_W):
            conv = conv + xp_ref[8 - (CONV_W - 1) + j:8 - (CONV_W - 1) + j + q, sl] * cw_ref[j:j + 1, sl]
        u_ref[:, sl] = conv * jax.nn.sigmoid(conv)

    @pl.when(c == pl.num_programs(1) - 1)
    def _():
        cn_ref[...] = xp_ref[8 + q - (CONV_W - 1):8 + q, :]

    xp_ref[0:8, :] = xp_ref[q:q + 8, :]

    lane = lax.broadcasted_iota(I32, (q, LANES), 1)
    head_lane = (lane >= SSD_HEAD_LANE) & (lane < SSD_HEAD_LANE + SSM_HEADS)
    pre = sm_ref[...] + dtb_ref[...]
    dt = jnp.where(head_lane, jnp.maximum(pre, 0.0) + jnp.log1p(jnp.exp(-jnp.abs(pre))), 0.0)
    ad = dt * (-jnp.exp(alog_ref[...]))
    row = lax.broadcasted_iota(I32, (q, q), 0)
    col = lax.broadcasted_iota(I32, (q, q), 1)
    tril = row >= col
    tri = jnp.where(tril, 1.0, 0.0).astype(BF16)
    acs = sum(jnp.dot(tri, part, preferred_element_type=F32) for part in _split3(ad))
    acs_last = acs[q - 1:q, :]
    expand = exp_ref[...]
    e1 = jnp.dot(jnp.exp(acs).astype(BF16), expand, preferred_element_type=F32)
    e2 = jnp.dot((jnp.exp(acs_last - acs) * dt).astype(BF16), expand, preferred_element_type=F32)
    dec8 = jnp.broadcast_to(jnp.exp(acs_last), (8, LANES))
    dec = sum(jnp.dot(part, expand, preferred_element_type=F32) for part in _split3(dec8))[0:1, :]
    acs_t = acs.T
    dt_t = dt.T
    lane_lo = lane < SSM_HEAD_DIM

    for g in range(SSM_GROUPS):
        gs = slice(g * GROUP_W, (g + 1) * GROUP_W)
        bg = u_ref[:, XS_W + g * D_STATE:XS_W + (g + 1) * D_STATE]
        cg = u_ref[:, XS_W + BC_W // 2 + g * D_STATE:XS_W + BC_W // 2 + (g + 1) * D_STATE].astype(BF16)
        cb = lax.dot_general(cg, bg.astype(BF16), NT_DIMS, preferred_element_type=F32)
        xs_g = u_ref[:, gs]
        y_diag = []
        for jp in range(SSM_HPG // 2):
            ms = []
            for r in (g * SSM_HPG + 2 * jp, g * SSM_HPG + 2 * jp + 1):
                hl = SSD_HEAD_LANE + r
                seg = acs[:, hl:hl + 1] - acs_t[hl:hl + 1, :]
                lmat = jnp.where(tril, jnp.exp(seg), 0.0)
                ms.append((cb * lmat * dt_t[hl:hl + 1, :]).astype(BF16))
            blk = xs_g[:, jp * LANES:(jp + 1) * LANES]
            rhs = jnp.concatenate([jnp.where(lane_lo, blk, 0.0), jnp.where(lane_lo, 0.0, blk)], axis=0)
            y_diag.append(jnp.dot(jnp.concatenate(ms, axis=1), rhs.astype(BF16), preferred_element_type=F32))
        h_old = ht_ref[:, gs]
        y_off = jnp.dot(cg, h_old.astype(BF16), preferred_element_type=F32) * e1[:, gs]
        y = jnp.concatenate(y_diag, axis=1) + y_off + dskip_ref[:, gs] * xs_g
        zz = z_ref[:, gs]
        y = y * (zz * jax.nn.sigmoid(zz))
        ms_y = jnp.sum(y * y, axis=-1, keepdims=True) * np.float32(1.0 / GROUP_W)
        y_ref[:, gs] = (y * lax.rsqrt(ms_y + EPS) * ng_ref[:, gs]).astype(BF16)
        upd = jnp.dot(bg.T.astype(BF16), (xs_g * e2[:, gs]).astype(BF16), preferred_element_type=F32)
        ht_ref[:, gs] = h_old * dec[:, gs] + upd

    @pl.when(c == pl.num_programs(1) - 1)
    def _():
        for j in range(D_INNER // LANES):
            h_ref[j * LANES:(j + 1) * LANES, :] = ht_ref[:, j * LANES:(j + 1) * LANES].T


def ssd_prompt(proj, conv_w, conv_b, dt_bias, a_log, d_skip, norm_g, bsz, seq):
    nc = seq // SSD_CHUNK
    n = bsz * seq
    rmap = lambda blk: (lambda b, c: (b * nc + c, blk))
    const = lambda a: pl.BlockSpec(a.shape, lambda b, c: (0, 0))
    consts = _ssd_consts(conv_w, conv_b, dt_bias, a_log, d_skip, norm_g)
    return pl.pallas_call(
        _ssd_kernel,
        out_shape=(jax.ShapeDtypeStruct((n, D_INNER), BF16),
                   jax.ShapeDtypeStruct((bsz, D_INNER, D_STATE), F32),
                   jax.ShapeDtypeStruct((bsz, CONV_W - 1, CONV_DIM), F32)),
        grid=(bsz, nc),
        in_specs=[pl.BlockSpec((SSD_CHUNK, D_INNER), rmap(OFF_Z // D_INNER)),
                  pl.BlockSpec((SSD_CHUNK, XS_W), rmap(OFF_XBC // XS_W)),
                  pl.BlockSpec((SSD_CHUNK, BC_W), rmap((OFF_XBC + XS_W) // BC_W)),
                  pl.BlockSpec((SSD_CHUNK, LANES), rmap(OFF_SMALL // LANES))] + [const(a) for a in consts],
        out_specs=(pl.BlockSpec((SSD_CHUNK, D_INNER), rmap(0)),
                   pl.BlockSpec((None, D_INNER, D_STATE), lambda b, c: (b, 0, 0)),
                   pl.BlockSpec((None, CONV_W - 1, CONV_DIM), lambda b, c: (b, 0, 0))),
        scratch_shapes=[pltpu.VMEM((SSD_CHUNK + 8, CONV_DIM), F32),
                        pltpu.VMEM((SSD_CHUNK, CONV_DIM), F32),
                        pltpu.VMEM((D_STATE, D_INNER), F32)],
        compiler_params=pltpu.CompilerParams(dimension_semantics=("parallel", "arbitrary"),
                                             vmem_limit_bytes=VMEM_LIMIT),
        name="ssd_prompt",
    )(proj, proj, proj, proj, *consts)


def _mix_kernel(ya_ref, yb_ref, ga_ref, gb_ref, x_ref, wa_ref, wb_ref, wo_ref, o_ref):
    ma = jnp.dot(ya_ref[...].astype(BF16), wa_ref[...], preferred_element_type=F32)
    mb = jnp.dot(yb_ref[...].astype(BF16), wb_ref[...], preferred_element_type=F32)
    mixed = jax.nn.sigmoid(ga_ref[...]) * ma + jax.nn.sigmoid(gb_ref[...]) * mb
    o_ref[...] = x_ref[...] + jnp.dot(mixed.astype(BF16), wo_ref[...], preferred_element_type=F32)


def branch_mix(ya, yb, proj, x, wa, wb, wo):
    n = x.shape[0]
    tm = _pick_tile(n, (256, 128))
    rows = lambda w, blk=0: pl.BlockSpec((tm, w), lambda i: (i, blk))
    whole = lambda a: pl.BlockSpec(a.shape, lambda i: (0, 0))
    return pl.pallas_call(
        _mix_kernel,
        out_shape=jax.ShapeDtypeStruct((n, D_MODEL), F32),
        grid=(n // tm,),
        in_specs=[rows(D_INNER), rows(ATT_WIDTH), rows(D_MODEL, OFF_GA // D_MODEL),
                  rows(D_MODEL, OFF_GB // D_MODEL), rows(D_MODEL), whole(wa), whole(wb), whole(wo)],
        out_specs=rows(D_MODEL),
        compiler_params=pltpu.CompilerParams(dimension_semantics=("parallel",),
                                             vmem_limit_bytes=VMEM_LIMIT),
        name="branch_mix",
    )(ya, yb, proj, proj, x, wa, wb, wo)


def _dsa_prompt_body(s_len, qb, iq_ref, sm_ref, ik_ref, q_ref, k_ref, v_ref, o_ref, jc_ref):
    ik = ik_ref[0:s_len, :].astype(BF16)
    iw = sm_ref[...].T[SMALL_IW:SMALL_IW + IDX_HEADS, :] * np.float32(IDX_SCALE)
    score = jnp.zeros((s_len, Q_BLOCK), F32)
    for h in range(IDX_HEADS):
        iqh = iq_ref[:, h * IDX_DIM:(h + 1) * IDX_DIM].astype(BF16)
        d = lax.dot_general(ik, iqh, NT_DIMS, preferred_element_type=F32)
        score = score + jnp.maximum(d, 0.0) * iw[h:h + 1, :]

    tpos = qb * Q_BLOCK + lax.broadcasted_iota(I32, (1, Q_BLOCK), 1)
    spos = lax.broadcasted_iota(I32, (s_len, 1), 0)
    causal = spos <= tpos
    bits = lax.bitcast_convert_type(score, I32)
    key = bits ^ ((bits >> 31) & np.int32(0x7FFFFFFF))
    key = jnp.where(score == 0.0, 0, key)
    key = jnp.where(causal, key, INT_MIN)

    def count(mask):
        return _tree_sum(mask.astype(I32), axis=0)

    def bisect_value(i, tu):
        cand_u = tu | jnp.left_shift(np.int32(1), 31 - i)
        return jnp.where(count(key >= (cand_u ^ INT_MIN)) >= TOPK_MAX, cand_u, tu)

    thr = lax.fori_loop(0, 32, bisect_value, jnp.zeros((1, Q_BLOCK), I32)) ^ INT_MIN
    above = key > thr
    tied = key == thr
    need = TOPK_MAX - count(above)
    excess = (count(tied) > need) & (thr != INT_MIN)

    jc_ref[...] = jnp.full(jc_ref.shape, s_len, I32)

    @pl.when(jnp.max(excess.astype(I32)) > 0)
    def _():
        def bisect_pos(i, j):
            cj = j | jnp.left_shift(np.int32(1), 10 - i)
            return jnp.where(count(tied & (spos < cj)) < need, cj, j)

        jc = lax.fori_loop(0, 11, bisect_pos, jnp.zeros((1, Q_BLOCK), I32))
        jc_ref[...] = jnp.broadcast_to(jc, jc_ref.shape)

    jc = jc_ref[0:1, :]
    sel = causal & (above | (tied & (spos <= jc)))
    bias_t = jnp.where(sel, 0.0, -jnp.inf)
    bias = jnp.concatenate([bias_t[i * Q_BLOCK:(i + 1) * Q_BLOCK, :].T for i in range(s_len // Q_BLOCK)], axis=1)

    for g in range(KV_HEADS):
        qs = jnp.concatenate(
            [q_ref[:, (ATT_REP * g + r) * HEAD_DIM:(ATT_REP * g + r + 1) * HEAD_DIM] for r in range(ATT_REP)],
            axis=0).astype(BF16)
        kg = k_ref[0:s_len, g * HEAD_DIM:(g + 1) * HEAD_DIM].astype(BF16)
        vg = v_ref[0:s_len, g * HEAD_DIM:(g + 1) * HEAD_DIM].astype(BF16)
        s = lax.dot_general(qs, kg, NT_DIMS, preferred_element_type=F32)
        s = (s.reshape(ATT_REP, Q_BLOCK, s_len) + bias[None]).reshape(ATT_REP * Q_BLOCK, s_len)
        m = jnp.max(s, axis=-1, keepdims=True)
        p = jnp.exp(s - m)
        l = jnp.sum(p, axis=-1, keepdims=True)
        o = jnp.dot(p.astype(BF16), vg, preferred_element_type=F32) / l
        for r in range(ATT_REP):
            hh = ATT_REP * g + r
            o_ref[:, hh * HEAD_DIM:(hh + 1) * HEAD_DIM] = o[r * Q_BLOCK:(r + 1) * Q_BLOCK]


DSA_KEY_BUCKET = 512


def _dsa_prompt_kernel(*refs):
    qb = pl.program_id(1)
    s_full = refs[2].shape[0]
    per = DSA_KEY_BUCKET // Q_BLOCK
    for j in range(s_full // DSA_KEY_BUCKET):
        pl.when((qb >= per * j) & (qb < per * (j + 1)))(
            functools.partial(_dsa_prompt_body, DSA_KEY_BUCKET * (j + 1), qb, *refs))


def dsa_prompt(q, k, v, iq, ik, proj, bsz, seq):
    nq = seq // Q_BLOCK
    assert seq == 2048 and TOPK_MAX <= seq // 4
    qmap = lambda b, i: (b * nq + i, 0)
    bmap = lambda b, i: (b, 0)
    return pl.pallas_call(
        _dsa_prompt_kernel,
        out_shape=jax.ShapeDtypeStruct((bsz * seq, ATT_WIDTH), F32),
        grid=(bsz, nq),
        in_specs=[pl.BlockSpec((Q_BLOCK, IDX_HEADS * IDX_DIM), qmap),
                  pl.BlockSpec((Q_BLOCK, LANES), lambda b, i: (b * nq + i, OFF_SMALL // LANES)),
                  pl.BlockSpec((seq, IDX_DIM), bmap),
                  pl.BlockSpec((Q_BLOCK, ATT_WIDTH), qmap),
                  pl.BlockSpec((seq, KV_WIDTH), bmap),
                  pl.BlockSpec((seq, KV_WIDTH), bmap)],
        out_specs=pl.BlockSpec((Q_BLOCK, ATT_WIDTH), qmap),
        scratch_shapes=[pltpu.VMEM((8, LANES), I32)],
        compiler_params=pltpu.CompilerParams(dimension_semantics=("parallel", "arbitrary"),
                                             vmem_limit_bytes=VMEM_LIMIT),
        name="dsa_prompt",
    )(iq, proj, ik, q, k, v)


def _column_tile(row):
    d = row.shape[1]
    eye = lax.broadcasted_iota(I32, (d, d), 0) == lax.broadcasted_iota(I32, (d, d), 1)
    col = jnp.sum(jnp.where(eye, jnp.broadcast_to(row, (d, d)), 0.0), axis=1, keepdims=True)
    return jnp.where(lax.broadcasted_iota(I32, (d, LANES), 1) == 0, col, 0.0)


def _dsa_sample_kernel(layer, n_pages, pt_ref, iq_ref, iw_ref, ikn_ref, q_ref, kn_ref, vn_ref,
                       cik_hbm, ck_hbm, cv_hbm, o_ref, ikb, kb, vb, sems):
    b = pl.program_id(0)
    past = n_pages * PAGE_SIZE
    width = past + LANES

    streams = ((cik_hbm, ikb), (ck_hbm, kb), (cv_hbm, vb))

    def page_copy(j, which):
        src, dst = streams[which]
        cols = pl.ds(pl.multiple_of(j * PAGE_SIZE, PAGE_SIZE), PAGE_SIZE)
        return pltpu.make_async_copy(src.at[layer, pt_ref[b, j]], dst.at[..., cols], sems.at[which])

    for which in range(len(streams)):
        def start_page(j, carry, which=which):
            page_copy(j, which).start()
            return carry

        lax.fori_loop(0, n_pages, start_page, 0)

    ikb[:, past:width] = _column_tile(ikn_ref[...])
    for g in range(KV_HEADS):
        kb[g, :, past:width] = _column_tile(kn_ref[:, g * HEAD_DIM:(g + 1) * HEAD_DIM])
        vb[g, :, past:width] = _column_tile(vn_ref[:, g * HEAD_DIM:(g + 1) * HEAD_DIM])

    def wait_stream(which):
        def wait_page(j, carry):
            page_copy(j, which).wait()
            return carry
        lax.fori_loop(0, n_pages, wait_page, 0)

    wait_stream(0)
    iq16 = jnp.concatenate([iq_ref[...], jnp.zeros((BF16_ROWS - IDX_HEADS, IDX_DIM), F32)], axis=0)
    sc = jnp.dot(iq16.astype(BF16), ikb[...].astype(BF16), preferred_element_type=F32)[0:IDX_HEADS]
    score = jnp.sum(jnp.maximum(sc, 0.0) * (iw_ref[...] * np.float32(IDX_SCALE)), axis=0, keepdims=True)
    fw = -(-width // (8 * LANES)) * LANES
    score = jnp.concatenate([score, jnp.zeros((1, 8 * fw - width), F32)], axis=1)
    score = jnp.concatenate([score[:, r * fw:(r + 1) * fw] for r in range(8)], axis=0)
    pos = lax.broadcasted_iota(I32, (8, fw), 0) * fw + lax.broadcasted_iota(I32, (8, fw), 1)
    allowed = pos <= past
    bits = lax.bitcast_convert_type(score, I32)
    key = bits ^ ((bits >> 31) & np.int32(0x7FFFFFFF))
    key = jnp.where(score == 0.0, 0, key)
    key = jnp.where(allowed, key, INT_MIN)

    def count(mask):
        return jnp.sum(_tree_sum(mask.astype(I32), axis=1), axis=0, keepdims=True)

    def bisect_value(i, tu):
        cand_u = tu | jnp.left_shift(np.int32(1), 31 - i)
        return jnp.where(count(key >= (cand_u ^ INT_MIN)) >= TOPK_MAX, cand_u, tu)

    thr = lax.fori_loop(0, 32, bisect_value, jnp.zeros((1, 1), I32)) ^ INT_MIN
    above = key > thr
    tied = key == thr
    need = TOPK_MAX - count(above)
    pos_bits = int(8 * fw - 1).bit_length()

    def bisect_pos(i, j):
        cj = j | jnp.left_shift(np.int32(1), pos_bits - 1 - i)
        return jnp.where(count(tied & (pos < cj)) < need, cj, j)

    jc = lax.fori_loop(0, pos_bits, bisect_pos, jnp.zeros((1, 1), I32))
    sel = allowed & (above | (tied & (pos <= jc)))
    bias = jnp.where(sel, 0.0, -jnp.inf)
    bias = jnp.concatenate([bias[r:r + 1, :] for r in range(8)], axis=1)[:, 0:width]

    wait_stream(1)
    probs, sums = [], []
    for g in range(KV_HEADS):
        qg = jnp.concatenate(
            [q_ref[:, (ATT_REP * g + r) * HEAD_DIM:(ATT_REP * g + r + 1) * HEAD_DIM] for r in range(ATT_REP)]
            + [jnp.zeros((BF16_ROWS - ATT_REP, HEAD_DIM), F32)], axis=0).astype(BF16)
        s = jnp.dot(qg, kb[g].astype(BF16), preferred_element_type=F32) + bias
        p = jnp.exp(s - jnp.max(s, axis=-1, keepdims=True))
        sums.append(jnp.sum(p, axis=-1, keepdims=True))
        probs.append(p.astype(BF16))
    wait_stream(2)
    for g in range(KV_HEADS):
        o = lax.dot_general(probs[g], vb[g].astype(BF16), NT_DIMS, preferred_element_type=F32) / sums[g]
        for r in range(ATT_REP):
            hh = ATT_REP * g + r
            o_ref[:, hh * HEAD_DIM:(hh + 1) * HEAD_DIM] = o[r:r + 1]


def dsa_sample(layer, q, k, v, iq, ik, iw, ck_t, cv_t, cik_t, page_table):
    bs, n_pages = page_table.shape
    width = n_pages * PAGE_SIZE + LANES
    assert TOPK_MAX <= (n_pages * PAGE_SIZE + 1) // 4
    row = lambda w: pl.BlockSpec((None, 1, w), lambda b, pt: (b, 0, 0))
    anyspec = pl.BlockSpec(memory_space=pl.ANY)
    grid_spec = pltpu.PrefetchScalarGridSpec(
        num_scalar_prefetch=1,
        grid=(bs,),
        in_specs=[pl.BlockSpec((None, IDX_HEADS, IDX_DIM), lambda b, pt: (b, 0, 0)),
                  pl.BlockSpec((None, IDX_HEADS, 1), lambda b, pt: (b, 0, 0)),
                  row(IDX_DIM), row(ATT_WIDTH), row(KV_WIDTH), row(KV_WIDTH),
                  anyspec, anyspec, anyspec],
        out_specs=row(ATT_WIDTH),
        scratch_shapes=[pltpu.VMEM((IDX_DIM, width), F32),
                        pltpu.VMEM((KV_HEADS, HEAD_DIM, width), F32),
                        pltpu.VMEM((KV_HEADS, HEAD_DIM, width), F32),
                        pltpu.SemaphoreType.DMA((3,))])
    out = pl.pallas_call(
        functools.partial(_dsa_sample_kernel, layer, n_pages),
        out_shape=jax.ShapeDtypeStruct((bs, 1, ATT_WIDTH), F32),
        grid_spec=grid_spec,
        compiler_params=pltpu.CompilerParams(dimension_semantics=("arbitrary",),
                                             vmem_limit_bytes=VMEM_LIMIT),
        name="dsa_sample",
    )(page_table, iq.reshape(bs, IDX_HEADS, IDX_DIM), iw.reshape(bs, IDX_HEADS, 1),
      ik.reshape(bs, 1, IDX_DIM), q.reshape(bs, 1, ATT_WIDTH), k.reshape(bs, 1, KV_WIDTH),
      v.reshape(bs, 1, KV_WIDTH), cik_t, ck_t, cv_t)
    return out.reshape(bs, ATT_WIDTH)


def _row_to_cols(row):
    nblk = row.shape[1] // LANES
    stacked = jnp.concatenate([row[:, j * LANES:(j + 1) * LANES] for j in range(nblk)]
                              + [jnp.zeros((LANES - nblk, LANES), F32)], axis=0)
    return stacked.T


def _ssd_step_kernel(z_ref, xs_ref, bc_ref, sm_ref, cp_ref, h_ref, cw_ref, cb_ref, dtb_ref, alog_ref,
                     dskip_ref, ng_ref, exp_ref, y_ref, ho_ref, cn_ref):
    xbc = jnp.concatenate([xs_ref[...], bc_ref[...]], axis=1)
    conv = cb_ref[...] + xbc * cw_ref[CONV_W - 1:CONV_W, :]
    for j in range(CONV_W - 1):
        conv = conv + cp_ref[j:j + 1, :] * cw_ref[j:j + 1, :]
    cn_ref[0:CONV_W - 2, :] = cp_ref[1:CONV_W - 1, :]
    cn_ref[CONV_W - 2:CONV_W - 1, :] = xbc
    u = conv * jax.nn.sigmoid(conv)
    xs = u[:, 0:XS_W]

    lane = lax.broadcasted_iota(I32, (1, LANES), 1)
    head_lane = (lane >= SSD_HEAD_LANE) & (lane < SSD_HEAD_LANE + SSM_HEADS)
    pre = sm_ref[...] + dtb_ref[...]
    dt = jnp.where(head_lane, jnp.maximum(pre, 0.0) + jnp.log1p(jnp.exp(-jnp.abs(pre))), 0.0)
    decay = jnp.exp(dt * (-jnp.exp(alog_ref[...])))
    expand = exp_ref[...]
    widen = lambda r: sum(jnp.dot(part, expand, preferred_element_type=F32)
                          for part in _split3(jnp.broadcast_to(r, (8, LANES))))[0:1, :]
    xd_cols = _row_to_cols(xs * widen(dt))
    dec_cols = _row_to_cols(widen(decay))

    y_cols = jnp.zeros((LANES, LANES), F32)
    col_id = lax.broadcasted_iota(I32, (LANES, LANES), 1)
    for j in range(D_INNER // LANES):
        g = j * LANES // GROUP_W
        b_row = u[:, XS_W + g * D_STATE:XS_W + (g + 1) * D_STATE]
        c_row = u[:, XS_W + BC_W // 2 + g * D_STATE:XS_W + BC_W // 2 + (g + 1) * D_STATE]
        rows = slice(j * LANES, (j + 1) * LANES)
        h_new = h_ref[rows, :] * dec_cols[:, j:j + 1] + xd_cols[:, j:j + 1] * b_row
        ho_ref[rows, :] = h_new
        y_cols = jnp.where(col_id == j, jnp.sum(h_new * c_row, axis=1, keepdims=True), y_cols)
    y_rows = y_cols.T
    y = jnp.concatenate([y_rows[j:j + 1, :] for j in range(D_INNER // LANES)], axis=1)
    y = y + dskip_ref[...] * xs
    zz = z_ref[...]
    y = y * (zz * jax.nn.sigmoid(zz))
    for g in range(SSM_GROUPS):
        gs = slice(g * GROUP_W, (g + 1) * GROUP_W)
        yg = y[:, gs]
        ms = jnp.sum(yg * yg, axis=-1, keepdims=True) * np.float32(1.0 / GROUP_W)
        y_ref[:, gs] = (yg * lax.rsqrt(ms + EPS) * ng_ref[:, gs]).astype(BF16)


def _ssd_consts(conv_w, conv_b, dt_bias, a_log, d_skip, norm_g):
    head_row = lambda v: jnp.zeros((1, LANES), F32).at[0, SSD_HEAD_LANE:SSD_HEAD_LANE + SSM_HEADS].set(v)
    expand = (jnp.arange(LANES)[:, None] == SSD_HEAD_LANE + jnp.arange(D_INNER)[None, :] // SSM_HEAD_DIM).astype(BF16)
    return (conv_w, conv_b.reshape(1, CONV_DIM), head_row(dt_bias), head_row(a_log),
            jnp.repeat(d_skip, SSM_HEAD_DIM).reshape(1, D_INNER), norm_g.reshape(1, D_INNER), expand)


def ssd_step(proj, conv_state, ssm_state, layer, conv_w, conv_b, dt_bias, a_log, d_skip, norm_g):
    bs = proj.shape[0]
    proj3 = proj.reshape(bs, 1, IN_WIDTH_PAD)
    states = ssm_state.reshape(ssm_state.shape[0], bs, D_INNER, D_STATE)
    row = lambda w, off: pl.BlockSpec((None, 1, w), lambda b: (b, 0, off // w))
    const = lambda a: pl.BlockSpec(a.shape, lambda b: (0, 0))
    consts = _ssd_consts(conv_w, conv_b, dt_bias, a_log, d_skip, norm_g)
    y, h_new, conv_new = pl.pallas_call(
        _ssd_step_kernel,
        out_shape=(jax.ShapeDtypeStruct((bs, 1, D_INNER), BF16),
                   jax.ShapeDtypeStruct((bs, D_INNER, D_STATE), F32),
                   jax.ShapeDtypeStruct((bs, CONV_W - 1, CONV_DIM), F32)),
        grid=(bs,),
        in_specs=[row(D_INNER, OFF_Z), row(XS_W, OFF_XBC), row(BC_W, OFF_XBC + XS_W), row(LANES, OFF_SMALL),
                  pl.BlockSpec((None, CONV_W - 1, CONV_DIM), lambda b: (b, 0, 0)),
                  pl.BlockSpec((None, None, D_INNER, D_STATE), lambda b: (layer, b, 0, 0))]
                 + [const(a) for a in consts],
        out_specs=(pl.BlockSpec((None, 1, D_INNER), lambda b: (b, 0, 0)),
                   pl.BlockSpec((None, D_INNER, D_STATE), lambda b: (b, 0, 0)),
                   pl.BlockSpec((None, CONV_W - 1, CONV_DIM), lambda b: (b, 0, 0))),
        compiler_params=pltpu.CompilerParams(dimension_semantics=("parallel",),
                                             vmem_limit_bytes=VMEM_LIMIT),
        name="ssd_step",
    )(proj3, proj3, proj3, proj3, conv_state, states, *consts)
    return y.reshape(bs, D_INNER), h_new, conv_new


def _layer(x, pos, lw, conv_prefix, h0, paged):
    (norm1_g, w_in_bf, conv_w, conv_b, dt_bias, a_log, d_skip, ssm_norm_g, q_norm_g, k_norm_g,
     w_a_bf, w_b_bf, w_out_bf, norm2_g, wq_bf, keys_bf, ut_bf, v_bf) = lw
    bsz, t_len, _ = x.shape
    n = bsz * t_len
    x2 = x.reshape(n, D_MODEL)
    proj, _ = norm_matmul(x2, norm1_g, w_in_bf)
    cos_t, sin_t = rope_tables(pos)
    q, k, v, iq, ik = attn_prep(proj, cos_t, sin_t, q_norm_g, k_norm_g)
    ssd_params = (conv_w, conv_b, dt_bias, a_log, d_skip, ssm_norm_g)
    if paged is None:
        y_a, h_new, conv_new = ssd_prompt(proj, *ssd_params, bsz, t_len)
    else:
        assert t_len == 1
        y_a, h_new, conv_new = ssd_step(proj, conv_prefix, h0, paged[0], *ssd_params)
    k4 = k.reshape(bsz, t_len, KV_HEADS, HEAD_DIM)
    v4 = v.reshape(bsz, t_len, KV_HEADS, HEAD_DIM)
    ik3 = ik.reshape(bsz, t_len, IDX_DIM)
    if paged is None:
        y_b = dsa_prompt(q, k, v, iq, ik, proj, bsz, t_len)
    else:
        y_b = dsa_sample(paged[0], q, k, v, iq, ik, proj[:, OFF_SMALL + SMALL_IW:OFF_SMALL + SMALL_IW + IDX_HEADS],
                         *paged[1:])
    x2 = branch_mix(y_a.reshape(n, D_INNER), y_b, proj, x2, w_a_bf, w_b_bf, w_out_bf)
    x2 = peer_block(x2, norm2_g, wq_bf, keys_bf, ut_bf, v_bf)
    h_out = h_new.reshape(bsz, SSM_HEADS, SSM_HEAD_DIM, D_STATE)
    return x2.reshape(bsz, t_len, D_MODEL), k4, v4, ik3, h_out, conv_new


def kernel(x_prompt, x_sample, cache_k, cache_v, cache_idx_k, state_ssm, state_conv, page_table,
           norm1_g, w_in, conv_w, conv_b, dt_bias, a_log, d_skip, ssm_norm_g, q_norm_g, k_norm_g,
           w_branch_a, w_branch_b, w_out, norm2_g, peer_wq, peer_keys, peer_u, peer_v):
    bp, seq = x_prompt.shape[:2]
    bs, t_new = x_sample.shape[:2]
    past = page_table.shape[1] * PAGE_SIZE
    pos_p = jnp.arange(seq)
    pos_s = jnp.tile(past + jnp.arange(t_new), bs)
    yp, ys = x_prompt, x_sample
    outs_p, outs_s = [], []
    ck_t = jnp.transpose(cache_k, (0, 1, 3, 4, 2))
    cv_t = jnp.transpose(cache_v, (0, 1, 3, 4, 2))
    cik_t = jnp.transpose(cache_idx_k, (0, 1, 3, 2))
    sizes = dict(zip(("z", "xbc", "dt", "q", "k", "v", "iq", "ik", "iw", "ga", "gb"), IN_SIZES))
    starts = dict(zip(sizes, np.cumsum((0,) + IN_SIZES[:-1])))
    order = ("z", "xbc", "q", "k", "v", "iq", "ga", "gb", "ik", "dt", "iw")
    for l in range(DEPTH):
        w_bf = w_in[l].astype(BF16)
        cols = [w_bf[:, int(starts[s]):int(starts[s]) + sizes[s]] for s in order]
        used = sum(sizes.values())
        w_in_pad = jnp.concatenate(cols + [jnp.zeros((D_MODEL, IN_WIDTH_PAD - used), BF16)], axis=1)
        lw = (norm1_g[l], w_in_pad, conv_w[l], conv_b[l], dt_bias[l], a_log[l], d_skip[l],
              ssm_norm_g[l], q_norm_g[l], k_norm_g[l], w_branch_a[l].astype(BF16),
              w_branch_b[l].astype(BF16), w_out[l].astype(BF16), norm2_g[l],
              peer_wq[l].astype(BF16),
              peer_keys[l].astype(BF16).reshape(2 * PEER_HEADS, N_KEYS, PEER_HALF),
              peer_u[l].T.astype(BF16), peer_v[l].astype(BF16))
        yp, *rest = _layer(yp, pos_p, lw, None, None, None)
        outs_p.append(rest)
        ys, *rest = _layer(ys, pos_s, lw, state_conv[l], state_ssm, (l, ck_t, cv_t, cik_t, page_table))
        outs_s.append(rest)
    stack = lambda outs, i: jnp.stack([o[i] for o in outs])
    return (yp, ys,
            stack(outs_p, 0), stack(outs_p, 1), stack(outs_p, 2), stack(outs_p, 3), stack(outs_p, 4),
            stack(outs_s, 0), stack(outs_s, 1), stack(outs_s, 2), stack(outs_s, 3), stack(outs_s, 4))
```

```python
import functools
import math

import jax
import jax.numpy as jnp
import numpy as np
from jax import lax
from jax.experimental import pallas as pl
from jax.experimental.pallas import tpu as pltpu

D_MODEL = 1024
DEPTH = 2
PAGE_SIZE = 128
D_INNER = 2048
SSM_HEAD_DIM = 64
SSM_HEADS = 32
SSM_GROUPS = 4
SSM_HPG = 8
D_STATE = 128
CONV_W = 4
CONV_DIM = D_INNER + 2 * SSM_GROUPS * D_STATE
SSD_CHUNK = 128
ATT_HEADS = 16
KV_HEADS = 4
HEAD_DIM = 64
ATT_REP = 4
ATT_WIDTH = 1024
KV_WIDTH = 256
IDX_HEADS = 8
IDX_DIM = 64
IDX_SCALE = (IDX_HEADS * IDX_DIM) ** -0.5
TOPK_MAX = 256
Q_BLOCK = 128
ROPE_THETA = 10000.0
PEER_HEADS = 8
N_KEYS = 128
N_EXPERTS = N_KEYS * N_KEYS
PEER_KEY_DIM = 256
PEER_HALF = 128
PEER_TOPK = 16
PEER_PAIRS = PEER_HEADS * PEER_TOPK
EPS = 1e-6
IN_SIZES = (D_INNER, CONV_DIM, SSM_HEADS, ATT_WIDTH, KV_WIDTH, KV_WIDTH,
            IDX_HEADS * IDX_DIM, IDX_DIM, IDX_HEADS, D_MODEL, D_MODEL)
IN_WIDTH = sum(IN_SIZES)
OFF_Z, OFF_XBC, OFF_Q, OFF_KV, OFF_IQ, OFF_GA, OFF_GB, OFF_SMALL = 0, 2048, 5120, 6144, 6656, 7168, 8192, 9216
SMALL_IK, SMALL_DT, SMALL_IW = 0, 64, 96
IN_WIDTH_PAD = 9728

F32 = jnp.float32
BF16 = jnp.bfloat16
I32 = jnp.int32
INT_MIN = np.int32(-2 ** 31)
VMEM_LIMIT = 56 * 1024 * 1024
LANES = 128
BF16_ROWS = 16
NT_DIMS = (((1,), (1,)), ((), ()))


def _tree_sum(x, axis):
    step = 8 if axis == 0 else LANES
    size = x.shape[axis]
    take = (lambda i: x[i:i + step, :]) if axis == 0 else (lambda i: x[:, i:i + step])
    parts = [take(i) for i in range(0, size, step)]
    while len(parts) > 1:
        pairs = [parts[i] + parts[i + 1] for i in range(0, len(parts) - 1, 2)]
        parts = pairs + ([parts[-1]] if len(parts) % 2 else [])
    return jnp.sum(parts[0], axis=axis, keepdims=True)


def _pick_tile(n, candidates):
    for c in candidates:
        if n % c == 0:
            return c
    return n


def _mm_norm_kernel(x_ref, g_ref, w_ref, o_ref, xn_ref):
    @pl.when(pl.program_id(1) == 0)
    def _():
        x = x_ref[...]
        ms = jnp.mean(x * x, axis=-1, keepdims=True)
        xn_ref[...] = (x * lax.rsqrt(ms + EPS) * g_ref[...]).astype(BF16)

    o_ref[...] = jnp.dot(xn_ref[...], w_ref[...], preferred_element_type=F32)


def norm_matmul(x, gain, w):
    m, k = x.shape
    n = w.shape[1]
    tm = _pick_tile(m, (2048, 1024, 512, 256, 128))
    tn = _pick_tile(n, (512, 256, 128))
    return pl.pallas_call(
        _mm_norm_kernel,
        out_shape=(jax.ShapeDtypeStruct((m, n), F32), jax.ShapeDtypeStruct((m, k), BF16)),
        grid=(m // tm, n // tn),
        in_specs=[pl.BlockSpec((tm, k), lambda i, j: (i, 0)),
                  pl.BlockSpec((1, k), lambda i, j: (0, 0)),
                  pl.BlockSpec((k, tn), lambda i, j: (0, j))],
        out_specs=(pl.BlockSpec((tm, tn), lambda i, j: (i, j)),
                   pl.BlockSpec((tm, k), lambda i, j: (i, 0))),
        compiler_params=pltpu.CompilerParams(dimension_semantics=("parallel", "arbitrary"),
                                             vmem_limit_bytes=VMEM_LIMIT),
        name="norm_matmul",
    )(x, gain.reshape(1, k), w)


def _top16_rows(s):
    r = s.shape[0]
    rows = lax.broadcasted_iota(I32, s.shape, 0)
    vals, idxs = [], []
    for _ in range(PEER_TOPK):
        m = jnp.max(s, axis=0, keepdims=True)
        i = jnp.min(jnp.where(s == m, rows, r), axis=0, keepdims=True)
        vals.append(m)
        idxs.append(i)
        s = jnp.where(rows == i, -jnp.inf, s)
    return jnp.concatenate(vals, axis=0), jnp.concatenate(idxs, axis=0)


def _peer_pairs(sv0, si0, sv1, si1):
    t = sv0.shape[1]
    row8 = lax.broadcasted_iota(I32, (8, t), 0)
    row16 = lax.broadcasted_iota(I32, (16, t), 0)
    vals = [sv0[0:1] + sv1]
    eids = [si0[0:1] * N_KEYS + si1]
    flat = [row16]
    for a in range(1, 8):
        nb = PEER_TOPK // (a + 1)
        v = sv0[a:a + 1] + sv1[0:8]
        vals.append(v if nb >= 8 else jnp.where(row8 < nb, v, -jnp.inf))
        eids.append(si0[a:a + 1] * N_KEYS + si1[0:8])
        flat.append(row8 + PEER_TOPK * a)
    vals.append(sv0[8:16] + sv1[0:1])
    eids.append(si0[8:16] * N_KEYS + si1[0:1])
    flat.append((row8 + 8) * PEER_TOPK)
    cand = jnp.concatenate(vals, axis=0)
    ce = jnp.concatenate(eids, axis=0)
    cf = jnp.concatenate(flat, axis=0)
    fv, fe = [], []
    for _ in range(PEER_TOPK):
        m = jnp.max(cand, axis=0, keepdims=True)
        pick = jnp.min(jnp.where(cand == m, cf, PEER_TOPK * PEER_TOPK), axis=0, keepdims=True)
        sel = cf == pick
        fe.append(jnp.max(jnp.where(sel, ce, -1), axis=0, keepdims=True))
        fv.append(m)
        cand = jnp.where(sel, -jnp.inf, cand)
    fv = jnp.concatenate(fv, axis=0)
    fe = jnp.concatenate(fe, axis=0)
    p = jnp.exp(fv - fv[0:1])
    return fe, p / jnp.sum(p, axis=0, keepdims=True)


def _peer_topk_kernel(q_ref, keys_ref, a_ref, b_ref, g_ref):
    q = q_ref[...].astype(BF16)
    es, gs = [], []
    for h in range(PEER_HEADS):
        tops = []
        for c in range(2):
            hc = 2 * h + c
            s = lax.dot_general(keys_ref[hc], q[:, hc * PEER_HALF:(hc + 1) * PEER_HALF], NT_DIMS,
                                preferred_element_type=F32)
            tops.append(_top16_rows(s))
        e, g = _peer_pairs(tops[0][0], tops[0][1], tops[1][0], tops[1][1])
        es.append(e)
        gs.append(g)
    e = jnp.concatenate(es, axis=0)
    g = jnp.concatenate(gs, axis=0)
    a_ref[...] = (e >> 7).astype(F32).T
    b_ref[...] = (e & (N_KEYS - 1)).astype(F32).T
    g_ref[...] = g.T


PEER_TOPK_TILE = 128


def peer_topk(q, keys_bf):
    n = q.shape[0]
    tt = PEER_TOPK_TILE
    spec = pl.BlockSpec((tt, PEER_PAIRS), lambda i: (i, 0))
    shp = jax.ShapeDtypeStruct((n, PEER_PAIRS), F32)
    return pl.pallas_call(
        _peer_topk_kernel,
        out_shape=(shp, shp, shp),
        grid=(n // tt,),
        in_specs=[pl.BlockSpec((tt, PEER_HEADS * PEER_KEY_DIM), lambda i: (i, 0)),
                  pl.BlockSpec((2 * PEER_HEADS, N_KEYS, PEER_HALF), lambda i: (0, 0, 0))],
        out_specs=(spec, spec, spec),
        compiler_params=pltpu.CompilerParams(dimension_semantics=("parallel",),
                                             vmem_limit_bytes=VMEM_LIMIT),
        name="peer_topk",
    )(q, keys_bf)


W_PITCH = 136
PEER_W_TILE = 64


def _peer_w_kernel(a_ref, b_ref, g_ref, o_ref, wf_ref):
    tt = a_ref.shape[0]
    ii = lax.broadcasted_iota(I32, (N_KEYS, PEER_PAIRS), 0).astype(F32)

    def body(t, carry):
        a = a_ref[pl.ds(t, 1), :]
        b = b_ref[pl.ds(t, 1), :]
        g = g_ref[pl.ds(t, 1), :]
        at = jnp.where(ii == a, 1.0, 0.0).astype(BF16)
        bt = jnp.where(ii == b, g, 0.0).astype(BF16)
        w = lax.dot_general(at, bt, NT_DIMS, preferred_element_type=F32)
        wf_ref[pl.ds(pl.multiple_of(t * W_PITCH, 8), N_KEYS), :] = w
        return carry

    lax.fori_loop(0, tt, body, 0, unroll=16)
    for i1 in range(N_KEYS):
        o_ref[:, i1 * N_KEYS:(i1 + 1) * N_KEYS] = wf_ref[pl.ds(i1, tt, stride=W_PITCH), :].astype(BF16)


def peer_gate_matrix(a, b, g):
    n = a.shape[0]
    tt = PEER_W_TILE
    spec = pl.BlockSpec((tt, PEER_PAIRS), lambda i: (i, 0))
    return pl.pallas_call(
        _peer_w_kernel,
        out_shape=jax.ShapeDtypeStruct((n, N_EXPERTS), BF16),
        grid=(n // tt,),
        in_specs=[spec, spec, spec],
        out_specs=pl.BlockSpec((tt, N_EXPERTS), lambda i: (i, 0)),
        scratch_shapes=[pltpu.VMEM((tt * W_PITCH, N_KEYS), F32)],
        compiler_params=pltpu.CompilerParams(dimension_semantics=("parallel",),
                                             vmem_limit_bytes=VMEM_LIMIT),
        name="peer_gate_matrix",
    )(a, b, g)


def _peer_dense_kernel(xn_ref, x_ref, ut_ref, w_ref, v_ref, o_ref):
    @pl.when(pl.program_id(1) == 0)
    def _():
        o_ref[...] = x_ref[...]

    h = jnp.dot(xn_ref[...], ut_ref[...], preferred_element_type=F32)
    act = 0.5 * h * (1.0 + lax.erf(h * np.float32(math.sqrt(0.5))))
    wa = (w_ref[...].astype(F32) * act).astype(BF16)
    o_ref[...] += jnp.dot(wa, v_ref[...], preferred_element_type=F32)


PEER_EXPERT_TILE = 1024


def peer_dense(xn, x, ut, w, v):
    n = x.shape[0]
    tm = _pick_tile(n, (1024, 512, 256, 128))
    te = PEER_EXPERT_TILE
    return pl.pallas_call(
        _peer_dense_kernel,
        out_shape=jax.ShapeDtypeStruct((n, D_MODEL), F32),
        grid=(n // tm, N_EXPERTS // te),
        in_specs=[pl.BlockSpec((tm, D_MODEL), lambda i, j: (i, 0)),
                  pl.BlockSpec((tm, D_MODEL), lambda i, j: (i, 0)),
                  pl.BlockSpec((D_MODEL, te), lambda i, j: (0, j)),
                  pl.BlockSpec((tm, te), lambda i, j: (i, j)),
                  pl.BlockSpec((te, D_MODEL), lambda i, j: (j, 0))],
        out_specs=pl.BlockSpec((tm, D_MODEL), lambda i, j: (i, 0)),
        compiler_params=pltpu.CompilerParams(dimension_semantics=("parallel", "arbitrary"),
                                             vmem_limit_bytes=VMEM_LIMIT),
        name="peer_dense",
    )(xn, x, ut, w, v)


def peer_block(x2, norm2_g, wq_bf, keys_bf, ut_bf, v_bf):
    n = x2.shape[0]
    npad = -(-n // PEER_TOPK_TILE) * PEER_TOPK_TILE
    xp = jnp.pad(x2, ((0, npad - n), (0, 0))) if npad != n else x2
    q, xn = norm_matmul(xp, norm2_g, wq_bf)
    a, b, g = peer_topk(q, keys_bf)
    w = peer_gate_matrix(a, b, g)
    out = peer_dense(xn, xp, ut_bf, w, v_bf)
    return out[:n] if npad != n else out


def _group_sum64(xx, bd):
    hi = xx.astype(BF16)
    r1 = xx - hi.astype(F32)
    mid = r1.astype(BF16)
    lo = (r1 - mid.astype(F32)).astype(BF16)
    dot = lambda a: jnp.dot(a, bd, preferred_element_type=F32)
    return dot(hi) + dot(mid) + dot(lo)


def _rope128(y, c, s, lo_half):
    partner = jnp.where(lo_half, pltpu.roll(y, LANES - 32, axis=1), pltpu.roll(y, 32, axis=1))
    return y * c + partner * s


def _attn_prep_kernel(q_ref, kv_ref, iq_ref, sm_ref, cos_ref, sin_ref, qg_ref, kg_ref,
                      qo_ref, ko_ref, vo_ref, iqo_ref, iko_ref):
    tm = q_ref.shape[0]
    lane = lax.broadcasted_iota(I32, (tm, LANES), 1)
    lo_half = (lane & (HEAD_DIM - 1)) < HEAD_DIM // 2
    grp_r = lax.broadcasted_iota(I32, (LANES, LANES), 0) >> 6
    grp_c = lax.broadcasted_iota(I32, (LANES, LANES), 1) >> 6
    bd = jnp.where(grp_r == grp_c, 1.0, 0.0).astype(BF16)
    c = cos_ref[...]
    s = sin_ref[...]

    def normed(x, g):
        ms = _group_sum64(x * x, bd) * np.float32(1.0 / HEAD_DIM)
        return x * lax.rsqrt(ms + EPS) * g

    for j in range(ATT_WIDTH // LANES):
        sl = slice(j * LANES, (j + 1) * LANES)
        qo_ref[:, sl] = _rope128(normed(q_ref[:, sl], qg_ref[...]), c, s, lo_half) * np.float32(HEAD_DIM ** -0.5)
    for j in range(KV_WIDTH // LANES):
        sl = slice(j * LANES, (j + 1) * LANES)
        ko_ref[:, sl] = _rope128(normed(kv_ref[:, sl], kg_ref[...]), c, s, lo_half)
    vo_ref[...] = kv_ref[:, KV_WIDTH:2 * KV_WIDTH]
    for j in range(IDX_HEADS * IDX_DIM // LANES):
        sl = slice(j * LANES, (j + 1) * LANES)
        iqo_ref[:, sl] = _rope128(iq_ref[:, sl], c, s, lo_half)
    iko_ref[...] = _rope128(sm_ref[...], c, s, lo_half)[:, SMALL_IK:SMALL_IK + IDX_DIM]


def attn_prep(proj, cos_t, sin_t, q_norm_g, k_norm_g):
    n = proj.shape[0]
    tm = _pick_tile(n, (256, 128))
    nt = cos_t.shape[0] // tm
    row = lambda w, off: pl.BlockSpec((tm, w), lambda i: (i, off // w))
    tab = pl.BlockSpec((tm, LANES), lambda i: (i % nt, 0))
    gain = pl.BlockSpec((1, LANES), lambda i: (0, 0))
    out = lambda w: pl.BlockSpec((tm, w), lambda i: (i, 0))
    shp = lambda w: jax.ShapeDtypeStruct((n, w), F32)
    tile2 = lambda g: jnp.tile(g.reshape(1, HEAD_DIM), (1, LANES // HEAD_DIM))
    return pl.pallas_call(
        _attn_prep_kernel,
        out_shape=(shp(ATT_WIDTH), shp(KV_WIDTH), shp(KV_WIDTH), shp(IDX_HEADS * IDX_DIM), shp(IDX_DIM)),
        grid=(n // tm,),
        in_specs=[row(ATT_WIDTH, OFF_Q), row(2 * KV_WIDTH, OFF_KV), row(IDX_HEADS * IDX_DIM, OFF_IQ),
                  row(LANES, OFF_SMALL), tab, tab, gain, gain],
        out_specs=(out(ATT_WIDTH), out(KV_WIDTH), out(KV_WIDTH), out(IDX_HEADS * IDX_DIM), out(IDX_DIM)),
        compiler_params=pltpu.CompilerParams(dimension_semantics=("parallel",),
                                             vmem_limit_bytes=VMEM_LIMIT),
        name="attn_prep",
    )(proj, proj, proj, proj, cos_t, sin_t, tile2(q_norm_g), tile2(k_norm_g))


def rope_tables(pos):
    inv = ROPE_THETA ** (-jnp.arange(0, HEAD_DIM, 2, dtype=F32) / HEAD_DIM)
    ang = pos.astype(F32)[:, None] * inv[None, :]
    cos, sin = jnp.cos(ang), jnp.sin(ang)
    reps = LANES // HEAD_DIM
    return (jnp.tile(jnp.concatenate([cos, cos], axis=-1), (1, reps)),
            jnp.tile(jnp.concatenate([-sin, sin], axis=-1), (1, reps)))


SSD_HEAD_LANE = SMALL_DT
XS_W = D_INNER
BC_W = 2 * SSM_GROUPS * D_STATE
GROUP_W = SSM_HPG * SSM_HEAD_DIM
CONV_COL_CHUNK = 512


def _split3(x):
    hi = x.astype(BF16)
    r1 = x - hi.astype(F32)
    mid = r1.astype(BF16)
    return hi, mid, (r1 - mid.astype(F32)).astype(BF16)


def _ssd_kernel(z_ref, xs_ref, bc_ref, sm_ref, cw_ref, cb_ref, dtb_ref, alog_ref, dskip_ref, ng_ref,
                exp_ref, y_ref, h_ref, cn_ref, xp_ref, u_ref, ht_ref):
    c = pl.program_id(1)
    q = SSD_CHUNK

    @pl.when(c == 0)
    def _():
        xp_ref[0:8, :] = jnp.zeros((8, CONV_DIM), F32)
        ht_ref[...] = jnp.zeros(ht_ref.shape, F32)

    xp_ref[8:8 + q, 0:XS_W] = xs_ref[...]
    xp_ref[8:8 + q, XS_W:CONV_DIM] = bc_ref[...]
    for cc in range(CONV_DIM // CONV_COL_CHUNK):
        sl = slice(cc * CONV_COL_CHUNK, (cc + 1) * CONV_COL_CHUNK)
        conv = cb_ref[:, sl]
        for j in range(CONV_W):
            conv = conv + xp_ref[8 - (CONV_W - 1) + j:8 - (CONV_W - 1) + j + q, sl] * cw_ref[j:j + 1, sl]
        u_ref[:, sl] = conv * jax.nn.sigmoid(conv)

    @pl.when(c == pl.num_programs(1) - 1)
    def _():
        cn_ref[...] = xp_ref[8 + q - (CONV_W - 1):8 + q, :]

    xp_ref[0:8, :] = xp_ref[q:q + 8, :]

    lane = lax.broadcasted_iota(I32, (q, LANES), 1)
    head_lane = (lane >= SSD_HEAD_LANE) & (lane < SSD_HEAD_LANE + SSM_HEADS)
    pre = sm_ref[...] + dtb_ref[...]
    dt = jnp.where(head_lane, jnp.maximum(pre, 0.0) + jnp.log1p(jnp.exp(-jnp.abs(pre))), 0.0)
    ad = dt * (-jnp.exp(alog_ref[...]))
    row = lax.broadcasted_iota(I32, (q, q), 0)
    col = lax.broadcasted_iota(I32, (q, q), 1)
    tril = row >= col
    tri = jnp.where(tril, 1.0, 0.0).astype(BF16)
    acs = sum(jnp.dot(tri, part, preferred_element_type=F32) for part in _split3(ad))
    acs_last = acs[q - 1:q, :]
    expand = exp_ref[...]
    e1 = jnp.dot(jnp.exp(acs).astype(BF16), expand, preferred_element_type=F32)
    e2 = jnp.dot((jnp.exp(acs_last - acs) * dt).astype(BF16), expand, preferred_element_type=F32)
    dec8 = jnp.broadcast_to(jnp.exp(acs_last), (8, LANES))
    dec = sum(jnp.dot(part, expand, preferred_element_type=F32) for part in _split3(dec8))[0:1, :]
    acs_t = acs.T
    dt_t = dt.T
    lane_lo = lane < SSM_HEAD_DIM

    for g in range(SSM_GROUPS):
        gs = slice(g * GROUP_W, (g + 1) * GROUP_W)
        bg = u_ref[:, XS_W + g * D_STATE:XS_W + (g + 1) * D_STATE]
        cg = u_ref[:, XS_W + BC_W // 2 + g * D_STATE:XS_W + BC_W // 2 + (g + 1) * D_STATE].astype(BF16)
        cb = lax.dot_general(cg, bg.astype(BF16), NT_DIMS, preferred_element_type=F32)
        xs_g = u_ref[:, gs]
        y_diag = []
        for jp in range(SSM_HPG // 2):
            ms = []
            for r in (g * SSM_HPG + 2 * jp, g * SSM_HPG + 2 * jp + 1):
                hl = SSD_HEAD_LANE + r
                seg = acs[:, hl:hl + 1] - acs_t[hl:hl + 1, :]
                lmat = jnp.where(tril, jnp.exp(seg), 0.0)
                ms.append((cb * lmat * dt_t[hl:hl + 1, :]).astype(BF16))
            blk = xs_g[:, jp * LANES:(jp + 1) * LANES]
            rhs = jnp.concatenate([jnp.where(lane_lo, blk, 0.0), jnp.where(lane_lo, 0.0, blk)], axis=0)
            y_diag.append(jnp.dot(jnp.concatenate(ms, axis=1), rhs.astype(BF16), preferred_element_type=F32))
        h_old = ht_ref[:, gs]
        y_off = jnp.dot(cg, h_old.astype(BF16), preferred_element_type=F32) * e1[:, gs]
        y = jnp.concatenate(y_diag, axis=1) + y_off + dskip_ref[:, gs] * xs_g
        zz = z_ref[:, gs]
        y = y * (zz * jax.nn.sigmoid(zz))
        ms_y = jnp.sum(y * y, axis=-1, keepdims=True) * np.float32(1.0 / GROUP_W)
        y_ref[:, gs] = (y * lax.rsqrt(ms_y + EPS) * ng_ref[:, gs]).astype(BF16)
        upd = jnp.dot(bg.T.astype(BF16), (xs_g * e2[:, gs]).astype(BF16), preferred_element_type=F32)
        ht_ref[:, gs] = h_old * dec[:, gs] + upd

    @pl.when(c == pl.num_programs(1) - 1)
    def _():
        for j in range(D_INNER // LANES):
            h_ref[j * LANES:(j + 1) * LANES, :] = ht_ref[:, j * LANES:(j + 1) * LANES].T


def ssd_prompt(proj, conv_w, conv_b, dt_bias, a_log, d_skip, norm_g, bsz, seq):
    nc = seq // SSD_CHUNK
    n = bsz * seq
    rmap = lambda blk: (lambda b, c: (b * nc + c, blk))
    const = lambda a: pl.BlockSpec(a.shape, lambda b, c: (0, 0))
    consts = _ssd_consts(conv_w, conv_b, dt_bias, a_log, d_skip, norm_g)
    return pl.pallas_call(
        _ssd_kernel,
        out_shape=(jax.ShapeDtypeStruct((n, D_INNER), BF16),
                   jax.ShapeDtypeStruct((bsz, D_INNER, D_STATE), F32),
                   jax.ShapeDtypeStruct((bsz, CONV_W - 1, CONV_DIM), F32)),
        grid=(bsz, nc),
        in_specs=[pl.BlockSpec((SSD_CHUNK, D_INNER), rmap(OFF_Z // D_INNER)),
                  pl.BlockSpec((SSD_CHUNK, XS_W), rmap(OFF_XBC // XS_W)),
                  pl.BlockSpec((SSD_CHUNK, BC_W), rmap((OFF_XBC + XS_W) // BC_W)),
                  pl.BlockSpec((SSD_CHUNK, LANES), rmap(OFF_SMALL // LANES))] + [const(a) for a in consts],
        out_specs=(pl.BlockSpec((SSD_CHUNK, D_INNER), rmap(0)),
                   pl.BlockSpec((None, D_INNER, D_STATE), lambda b, c: (b, 0, 0)),
                   pl.BlockSpec((None, CONV_W - 1, CONV_DIM), lambda b, c: (b, 0, 0))),
        scratch_shapes=[pltpu.VMEM((SSD_CHUNK + 8, CONV_DIM), F32),
                        pltpu.VMEM((SSD_CHUNK, CONV_DIM), F32),
                        pltpu.VMEM((D_STATE, D_INNER), F32)],
        compiler_params=pltpu.CompilerParams(dimension_semantics=("parallel", "arbitrary"),
                                             vmem_limit_bytes=VMEM_LIMIT),
        name="ssd_prompt",
    )(proj, proj, proj, proj, *consts)


def _mix_kernel(ya_ref, yb_ref, ga_ref, gb_ref, x_ref, wa_ref, wb_ref, wo_ref, o_ref):
    ma = jnp.dot(ya_ref[...].astype(BF16), wa_ref[...], preferred_element_type=F32)
    mb = jnp.dot(yb_ref[...].astype(BF16), wb_ref[...], preferred_element_type=F32)
    mixed = jax.nn.sigmoid(ga_ref[...]) * ma + jax.nn.sigmoid(gb_ref[...]) * mb
    o_ref[...] = x_ref[...] + jnp.dot(mixed.astype(BF16), wo_ref[...], preferred_element_type=F32)


def branch_mix(ya, yb, proj, x, wa, wb, wo):
    n = x.shape[0]
    tm = _pick_tile(n, (256, 128))
    rows = lambda w, blk=0: pl.BlockSpec((tm, w), lambda i: (i, blk))
    whole = lambda a: pl.BlockSpec(a.shape, lambda i: (0, 0))
    return pl.pallas_call(
        _mix_kernel,
        out_shape=jax.ShapeDtypeStruct((n, D_MODEL), F32),
        grid=(n // tm,),
        in_specs=[rows(D_INNER), rows(ATT_WIDTH), rows(D_MODEL, OFF_GA // D_MODEL),
                  rows(D_MODEL, OFF_GB // D_MODEL), rows(D_MODEL), whole(wa), whole(wb), whole(wo)],
        out_specs=rows(D_MODEL),
        compiler_params=pltpu.CompilerParams(dimension_semantics=("parallel",),
                                             vmem_limit_bytes=VMEM_LIMIT),
        name="branch_mix",
    )(ya, yb, proj, proj, x, wa, wb, wo)


def _dsa_prompt_body(s_len, qb, iq_ref, sm_ref, ik_ref, q_ref, k_ref, v_ref, o_ref, jc_ref):
    ik = ik_ref[0:s_len, :].astype(BF16)
    iw = sm_ref[...].T[SMALL_IW:SMALL_IW + IDX_HEADS, :] * np.float32(IDX_SCALE)
    score = jnp.zeros((s_len, Q_BLOCK), F32)
    for h in range(IDX_HEADS):
        iqh = iq_ref[:, h * IDX_DIM:(h + 1) * IDX_DIM].astype(BF16)
        d = lax.dot_general(ik, iqh, NT_DIMS, preferred_element_type=F32)
        score = score + jnp.maximum(d, 0.0) * iw[h:h + 1, :]

    tpos = qb * Q_BLOCK + lax.broadcasted_iota(I32, (1, Q_BLOCK), 1)
    spos = lax.broadcasted_iota(I32, (s_len, 1), 0)
    causal = spos <= tpos
    bits = lax.bitcast_convert_type(score, I32)
    key = bits ^ ((bits >> 31) & np.int32(0x7FFFFFFF))
    key = jnp.where(score == 0.0, 0, key)
    key = jnp.where(causal, key, INT_MIN)

    def count(mask):
        return _tree_sum(mask.astype(I32), axis=0)

    def bisect_value(i, tu):
        cand_u = tu | jnp.left_shift(np.int32(1), 31 - i)
        return jnp.where(count(key >= (cand_u ^ INT_MIN)) >= TOPK_MAX, cand_u, tu)

    thr = lax.fori_loop(0, 32, bisect_value, jnp.zeros((1, Q_BLOCK), I32)) ^ INT_MIN
    above = key > thr
    tied = key == thr
    need = TOPK_MAX - count(above)
    excess = (count(tied) > need) & (thr != INT_MIN)

    jc_ref[...] = jnp.full(jc_ref.shape, s_len, I32)

    @pl.when(jnp.max(excess.astype(I32)) > 0)
    def _():
        def bisect_pos(i, j):
            cj = j | jnp.left_shift(np.int32(1), 10 - i)
            return jnp.where(count(tied & (spos < cj)) < need, cj, j)

        jc = lax.fori_loop(0, 11, bisect_pos, jnp.zeros((1, Q_BLOCK), I32))
        jc_ref[...] = jnp.broadcast_to(jc, jc_ref.shape)

    jc = jc_ref[0:1, :]
    sel = causal & (above | (tied & (spos <= jc)))
    bias_t = jnp.where(sel, 0.0, -jnp.inf)
    bias = jnp.concatenate([bias_t[i * Q_BLOCK:(i + 1) * Q_BLOCK, :].T for i in range(s_len // Q_BLOCK)], axis=1)

    for g in range(KV_HEADS):
        qs = jnp.concatenate(
            [q_ref[:, (ATT_REP * g + r) * HEAD_DIM:(ATT_REP * g + r + 1) * HEAD_DIM] for r in range(ATT_REP)],
            axis=0).astype(BF16)
        kg = k_ref[0:s_len, g * HEAD_DIM:(g + 1) * HEAD_DIM].astype(BF16)
        vg = v_ref[0:s_len, g * HEAD_DIM:(g + 1) * HEAD_DIM].astype(BF16)
        s = lax.dot_general(qs, kg, NT_DIMS, preferred_element_type=F32)
        s = (s.reshape(ATT_REP, Q_BLOCK, s_len) + bias[None]).reshape(ATT_REP * Q_BLOCK, s_len)
        m = jnp.max(s, axis=-1, keepdims=True)
        p = jnp.exp(s - m)
        l = jnp.sum(p, axis=-1, keepdims=True)
        o = jnp.dot(p.astype(BF16), vg, preferred_element_type=F32) / l
        for r in range(ATT_REP):
            hh = ATT_REP * g + r
            o_ref[:, hh * HEAD_DIM:(hh + 1) * HEAD_DIM] = o[r * Q_BLOCK:(r + 1) * Q_BLOCK]


DSA_KEY_BUCKET = 256


def _dsa_prompt_kernel(*refs):
    qb = pl.program_id(1)
    s_full = refs[2].shape[0]
    per = DSA_KEY_BUCKET // Q_BLOCK
    for j in range(s_full // DSA_KEY_BUCKET):
        pl.when((qb >= per * j) & (qb < per * (j + 1)))(
            functools.partial(_dsa_prompt_body, DSA_KEY_BUCKET * (j + 1), qb, *refs))


def dsa_prompt(q, k, v, iq, ik, proj, bsz, seq):
    nq = seq // Q_BLOCK
    assert seq == 2048 and TOPK_MAX <= seq // 4
    qmap = lambda b, i: (b * nq + i, 0)
    bmap = lambda b, i: (b, 0)
    return pl.pallas_call(
        _dsa_prompt_kernel,
        out_shape=jax.ShapeDtypeStruct((bsz * seq, ATT_WIDTH), F32),
        grid=(bsz, nq),
        in_specs=[pl.BlockSpec((Q_BLOCK, IDX_HEADS * IDX_DIM), qmap),
                  pl.BlockSpec((Q_BLOCK, LANES), lambda b, i: (b * nq + i, OFF_SMALL // LANES)),
                  pl.BlockSpec((seq, IDX_DIM), bmap),
                  pl.BlockSpec((Q_BLOCK, ATT_WIDTH), qmap),
                  pl.BlockSpec((seq, KV_WIDTH), bmap),
                  pl.BlockSpec((seq, KV_WIDTH), bmap)],
        out_specs=pl.BlockSpec((Q_BLOCK, ATT_WIDTH), qmap),
        scratch_shapes=[pltpu.VMEM((8, LANES), I32)],
        compiler_params=pltpu.CompilerParams(dimension_semantics=("parallel", "arbitrary"),
                                             vmem_limit_bytes=VMEM_LIMIT),
        name="dsa_prompt",
    )(iq, proj, ik, q, k, v)


def _column_tile(row):
    d = row.shape[1]
    eye = lax.broadcasted_iota(I32, (d, d), 0) == lax.broadcasted_iota(I32, (d, d), 1)
    col = jnp.sum(jnp.where(eye, jnp.broadcast_to(row, (d, d)), 0.0), axis=1, keepdims=True)
    return jnp.where(lax.broadcasted_iota(I32, (d, LANES), 1) == 0, col, 0.0)


def _dsa_sample_kernel(layer, n_pages, pt_ref, iq_ref, iw_ref, ikn_ref, q_ref, kn_ref, vn_ref,
                       cik_hbm, ck_hbm, cv_hbm, o_ref, ikb, kb, vb, sems):
    b = pl.program_id(0)
    past = n_pages * PAGE_SIZE
    width = past + LANES

    streams = ((cik_hbm, ikb), (ck_hbm, kb), (cv_hbm, vb))

    def page_copy(j, which):
        src, dst = streams[which]
        cols = pl.ds(pl.multiple_of(j * PAGE_SIZE, PAGE_SIZE), PAGE_SIZE)
        return pltpu.make_async_copy(src.at[layer, pt_ref[b, j]], dst.at[..., cols], sems.at[which])

    for which in range(len(streams)):
        def start_page(j, carry, which=which):
            page_copy(j, which).start()
            return carry

        lax.fori_loop(0, n_pages, start_page, 0)

    ikb[:, past:width] = _column_tile(ikn_ref[...])
    for g in range(KV_HEADS):
        kb[g, :, past:width] = _column_tile(kn_ref[:, g * HEAD_DIM:(g + 1) * HEAD_DIM])
        vb[g, :, past:width] = _column_tile(vn_ref[:, g * HEAD_DIM:(g + 1) * HEAD_DIM])

    def wait_stream(which):
        def wait_page(j, carry):
            page_copy(j, which).wait()
            return carry
        lax.fori_loop(0, n_pages, wait_page, 0)

    wait_stream(0)
    iq16 = jnp.concatenate([iq_ref[...], jnp.zeros((BF16_ROWS - IDX_HEADS, IDX_DIM), F32)], axis=0)
    sc = jnp.dot(iq16.astype(BF16), ikb[...].astype(BF16), preferred_element_type=F32)[0:IDX_HEADS]
    score = jnp.sum(jnp.maximum(sc, 0.0) * (iw_ref[...] * np.float32(IDX_SCALE)), axis=0, keepdims=True)
    fw = -(-width // (8 * LANES)) * LANES
    score = jnp.concatenate([score, jnp.zeros((1, 8 * fw - width), F32)], axis=1)
    score = jnp.concatenate([score[:, r * fw:(r + 1) * fw] for r in range(8)], axis=0)
    pos = lax.broadcasted_iota(I32, (8, fw), 0) * fw + lax.broadcasted_iota(I32, (8, fw), 1)
    allowed = pos <= past
    bits = lax.bitcast_convert_type(score, I32)
    key = bits ^ ((bits >> 31) & np.int32(0x7FFFFFFF))
    key = jnp.where(score == 0.0, 0, key)
    key = jnp.where(allowed, key, INT_MIN)

    def count(mask):
        return jnp.sum(_tree_sum(mask.astype(I32), axis=1), axis=0, keepdims=True)

    def bisect_value(i, tu):
        cand_u = tu | jnp.left_shift(np.int32(1), 31 - i)
        return jnp.where(count(key >= (cand_u ^ INT_MIN)) >= TOPK_MAX, cand_u, tu)

    thr = lax.fori_loop(0, 32, bisect_value, jnp.zeros((1, 1), I32)) ^ INT_MIN
    above = key > thr
    tied = key == thr
    need = TOPK_MAX - count(above)
    pos_bits = int(8 * fw - 1).bit_length()

    def bisect_pos(i, j):
        cj = j | jnp.left_shift(np.int32(1), pos_bits - 1 - i)
        return jnp.where(count(tied & (pos < cj)) < need, cj, j)

    jc = lax.fori_loop(0, pos_bits, bisect_pos, jnp.zeros((1, 1), I32))
    sel = allowed & (above | (tied & (pos <= jc)))
    bias = jnp.where(sel, 0.0, -jnp.inf)
    bias = jnp.concatenate([bias[r:r + 1, :] for r in range(8)], axis=1)[:, 0:width]

    wait_stream(1)
    probs, sums = [], []
    for g in range(KV_HEADS):
        qg = jnp.concatenate(
            [q_ref[:, (ATT_REP * g + r) * HEAD_DIM:(ATT_REP * g + r + 1) * HEAD_DIM] for r in range(ATT_REP)]
            + [jnp.zeros((BF16_ROWS - ATT_REP, HEAD_DIM), F32)], axis=0).astype(BF16)
        s = jnp.dot(qg, kb[g].astype(BF16), preferred_element_type=F32) + bias
        p = jnp.exp(s - jnp.max(s, axis=-1, keepdims=True))
        sums.append(jnp.sum(p, axis=-1, keepdims=True))
        probs.append(p.astype(BF16))
    wait_stream(2)
    for g in range(KV_HEADS):
        o = lax.dot_general(probs[g], vb[g].astype(BF16), NT_DIMS, preferred_element_type=F32) / sums[g]
        for r in range(ATT_REP):
            hh = ATT_REP * g + r
            o_ref[:, hh * HEAD_DIM:(hh + 1) * HEAD_DIM] = o[r:r + 1]


def dsa_sample(layer, q, k, v, iq, ik, iw, ck_t, cv_t, cik_t, page_table):
    bs, n_pages = page_table.shape
    width = n_pages * PAGE_SIZE + LANES
    assert TOPK_MAX <= (n_pages * PAGE_SIZE + 1) // 4
    row = lambda w: pl.BlockSpec((None, 1, w), lambda b, pt: (b, 0, 0))
    anyspec = pl.BlockSpec(memory_space=pl.ANY)
    grid_spec = pltpu.PrefetchScalarGridSpec(
        num_scalar_prefetch=1,
        grid=(bs,),
        in_specs=[pl.BlockSpec((None, IDX_HEADS, IDX_DIM), lambda b, pt: (b, 0, 0)),
                  pl.BlockSpec((None, IDX_HEADS, 1), lambda b, pt: (b, 0, 0)),
                  row(IDX_DIM), row(ATT_WIDTH), row(KV_WIDTH), row(KV_WIDTH),
                  anyspec, anyspec, anyspec],
        out_specs=row(ATT_WIDTH),
        scratch_shapes=[pltpu.VMEM((IDX_DIM, width), F32),
                        pltpu.VMEM((KV_HEADS, HEAD_DIM, width), F32),
                        pltpu.VMEM((KV_HEADS, HEAD_DIM, width), F32),
                        pltpu.SemaphoreType.DMA((3,))])
    out = pl.pallas_call(
        functools.partial(_dsa_sample_kernel, layer, n_pages),
        out_shape=jax.ShapeDtypeStruct((bs, 1, ATT_WIDTH), F32),
        grid_spec=grid_spec,
        compiler_params=pltpu.CompilerParams(dimension_semantics=("arbitrary",),
                                             vmem_limit_bytes=VMEM_LIMIT),
        name="dsa_sample",
    )(page_table, iq.reshape(bs, IDX_HEADS, IDX_DIM), iw.reshape(bs, IDX_HEADS, 1),
      ik.reshape(bs, 1, IDX_DIM), q.reshape(bs, 1, ATT_WIDTH), k.reshape(bs, 1, KV_WIDTH),
      v.reshape(bs, 1, KV_WIDTH), cik_t, ck_t, cv_t)
    return out.reshape(bs, ATT_WIDTH)


def _row_to_cols(row):
    nblk = row.shape[1] // LANES
    stacked = jnp.concatenate([row[:, j * LANES:(j + 1) * LANES] for j in range(nblk)]
                              + [jnp.zeros((LANES - nblk, LANES), F32)], axis=0)
    return stacked.T


def _ssd_step_kernel(z_ref, xs_ref, bc_ref, sm_ref, cp_ref, h_ref, cw_ref, cb_ref, dtb_ref, alog_ref,
                     dskip_ref, ng_ref, exp_ref, y_ref, ho_ref, cn_ref):
    xbc = jnp.concatenate([xs_ref[...], bc_ref[...]], axis=1)
    conv = cb_ref[...] + xbc * cw_ref[CONV_W - 1:CONV_W, :]
    for j in range(CONV_W - 1):
        conv = conv + cp_ref[j:j + 1, :] * cw_ref[j:j + 1, :]
    cn_ref[0:CONV_W - 2, :] = cp_ref[1:CONV_W - 1, :]
    cn_ref[CONV_W - 2:CONV_W - 1, :] = xbc
    u = conv * jax.nn.sigmoid(conv)
    xs = u[:, 0:XS_W]

    lane = lax.broadcasted_iota(I32, (1, LANES), 1)
    head_lane = (lane >= SSD_HEAD_LANE) & (lane < SSD_HEAD_LANE + SSM_HEADS)
    pre = sm_ref[...] + dtb_ref[...]
    dt = jnp.where(head_lane, jnp.maximum(pre, 0.0) + jnp.log1p(jnp.exp(-jnp.abs(pre))), 0.0)
    decay = jnp.exp(dt * (-jnp.exp(alog_ref[...])))
    expand = exp_ref[...]
    widen = lambda r: sum(jnp.dot(part, expand, preferred_element_type=F32)
                          for part in _split3(jnp.broadcast_to(r, (8, LANES))))[0:1, :]
    xd_cols = _row_to_cols(xs * widen(dt))
    dec_cols = _row_to_cols(widen(decay))

    y_cols = jnp.zeros((LANES, LANES), F32)
    col_id = lax.broadcasted_iota(I32, (LANES, LANES), 1)
    for j in range(D_INNER // LANES):
        g = j * LANES // GROUP_W
        b_row = u[:, XS_W + g * D_STATE:XS_W + (g + 1) * D_STATE]
        c_row = u[:, XS_W + BC_W // 2 + g * D_STATE:XS_W + BC_W // 2 + (g + 1) * D_STATE]
        rows = slice(j * LANES, (j + 1) * LANES)
        h_new = h_ref[rows, :] * dec_cols[:, j:j + 1] + xd_cols[:, j:j + 1] * b_row
        ho_ref[rows, :] = h_new
        y_cols = jnp.where(col_id == j, jnp.sum(h_new * c_row, axis=1, keepdims=True), y_cols)
    y_rows = y_cols.T
    y = jnp.concatenate([y_rows[j:j + 1, :] for j in range(D_INNER // LANES)], axis=1)
    y = y + dskip_ref[...] * xs
    zz = z_ref[...]
    y = y * (zz * jax.nn.sigmoid(zz))
    for g in range(SSM_GROUPS):
        gs = slice(g * GROUP_W, (g + 1) * GROUP_W)
        yg = y[:, gs]
        ms = jnp.sum(yg * yg, axis=-1, keepdims=True) * np.float32(1.0 / GROUP_W)
        y_ref[:, gs] = (yg * lax.rsqrt(ms + EPS) * ng_ref[:, gs]).astype(BF16)


def _ssd_consts(conv_w, conv_b, dt_bias, a_log, d_skip, norm_g):
    head_row = lambda v: jnp.zeros((1, LANES), F32).at[0, SSD_HEAD_LANE:SSD_HEAD_LANE + SSM_HEADS].set(v)
    expand = (jnp.arange(LANES)[:, None] == SSD_HEAD_LANE + jnp.arange(D_INNER)[None, :] // SSM_HEAD_DIM).astype(BF16)
    return (conv_w, conv_b.reshape(1, CONV_DIM), head_row(dt_bias), head_row(a_log),
            jnp.repeat(d_skip, SSM_HEAD_DIM).reshape(1, D_INNER), norm_g.reshape(1, D_INNER), expand)


def ssd_step(proj, conv_state, ssm_state, layer, conv_w, conv_b, dt_bias, a_log, d_skip, norm_g):
    bs = proj.shape[0]
    proj3 = proj.reshape(bs, 1, IN_WIDTH_PAD)
    states = ssm_state.reshape(ssm_state.shape[0], bs, D_INNER, D_STATE)
    row = lambda w, off: pl.BlockSpec((None, 1, w), lambda b: (b, 0, off // w))
    const = lambda a: pl.BlockSpec(a.shape, lambda b: (0, 0))
    consts = _ssd_consts(conv_w, conv_b, dt_bias, a_log, d_skip, norm_g)
    y, h_new, conv_new = pl.pallas_call(
        _ssd_step_kernel,
        out_shape=(jax.ShapeDtypeStruct((bs, 1, D_INNER), BF16),
                   jax.ShapeDtypeStruct((bs, D_INNER, D_STATE), F32),
                   jax.ShapeDtypeStruct((bs, CONV_W - 1, CONV_DIM), F32)),
        grid=(bs,),
        in_specs=[row(D_INNER, OFF_Z), row(XS_W, OFF_XBC), row(BC_W, OFF_XBC + XS_W), row(LANES, OFF_SMALL),
                  pl.BlockSpec((None, CONV_W - 1, CONV_DIM), lambda b: (b, 0, 0)),
                  pl.BlockSpec((None, None, D_INNER, D_STATE), lambda b: (layer, b, 0, 0))]
                 + [const(a) for a in consts],
        out_specs=(pl.BlockSpec((None, 1, D_INNER), lambda b: (b, 0, 0)),
                   pl.BlockSpec((None, D_INNER, D_STATE), lambda b: (b, 0, 0)),
                   pl.BlockSpec((None, CONV_W - 1, CONV_DIM), lambda b: (b, 0, 0))),
        compiler_params=pltpu.CompilerParams(dimension_semantics=("parallel",),
                                             vmem_limit_bytes=VMEM_LIMIT),
        name="ssd_step",
    )(proj3, proj3, proj3, proj3, conv_state, states, *consts)
    return y.reshape(bs, D_INNER), h_new, conv_new


def _layer(x, pos, lw, conv_prefix, h0, paged):
    (norm1_g, w_in_bf, conv_w, conv_b, dt_bias, a_log, d_skip, ssm_norm_g, q_norm_g, k_norm_g,
     w_a_bf, w_b_bf, w_out_bf, norm2_g, wq_bf, keys_bf, ut_bf, v_bf) = lw
    bsz, t_len, _ = x.shape
    n = bsz * t_len
    x2 = x.reshape(n, D_MODEL)
    proj, _ = norm_matmul(x2, norm1_g, w_in_bf)
    cos_t, sin_t = rope_tables(pos)
    q, k, v, iq, ik = attn_prep(proj, cos_t, sin_t, q_norm_g, k_norm_g)
    ssd_params = (conv_w, conv_b, dt_bias, a_log, d_skip, ssm_norm_g)
    if paged is None:
        y_a, h_new, conv_new = ssd_prompt(proj, *ssd_params, bsz, t_len)
    else:
        assert t_len == 1
        y_a, h_new, conv_new = ssd_step(proj, conv_prefix, h0, paged[0], *ssd_params)
    k4 = k.reshape(bsz, t_len, KV_HEADS, HEAD_DIM)
    v4 = v.reshape(bsz, t_len, KV_HEADS, HEAD_DIM)
    ik3 = ik.reshape(bsz, t_len, IDX_DIM)
    if paged is None:
        y_b = dsa_prompt(q, k, v, iq, ik, proj, bsz, t_len)
    else:
        y_b = dsa_sample(paged[0], q, k, v, iq, ik, proj[:, OFF_SMALL + SMALL_IW:OFF_SMALL + SMALL_IW + IDX_HEADS],
                         *paged[1:])
    x2 = branch_mix(y_a.reshape(n, D_INNER), y_b, proj, x2, w_a_bf, w_b_bf, w_out_bf)
    x2 = peer_block(x2, norm2_g, wq_bf, keys_bf, ut_bf, v_bf)
    h_out = h_new.reshape(bsz, SSM_HEADS, SSM_HEAD_DIM, D_STATE)
    return x2.reshape(bsz, t_len, D_MODEL), k4, v4, ik3, h_out, conv_new


def kernel(x_prompt, x_sample, cache_k, cache_v, cache_idx_k, state_ssm, state_conv, page_table,
           norm1_g, w_in, conv_w, conv_b, dt_bias, a_log, d_skip, ssm_norm_g, q_norm_g, k_norm_g,
           w_branch_a, w_branch_b, w_out, norm2_g, peer_wq, peer_keys, peer_u, peer_v):
    bp, seq = x_prompt.shape[:2]
    bs, t_new = x_sample.shape[:2]
    past = page_table.shape[1] * PAGE_SIZE
    pos_p = jnp.arange(seq)
    pos_s = jnp.tile(past + jnp.arange(t_new), bs)
    yp, ys = x_prompt, x_sample
    outs_p, outs_s = [], []
    ck_t = jnp.transpose(cache_k, (0, 1, 3, 4, 2))
    cv_t = jnp.transpose(cache_v, (0, 1, 3, 4, 2))
    cik_t = jnp.transpose(cache_idx_k, (0, 1, 3, 2))
    sizes = dict(zip(("z", "xbc", "dt", "q", "k", "v", "iq", "ik", "iw", "ga", "gb"), IN_SIZES))
    starts = dict(zip(sizes, np.cumsum((0,) + IN_SIZES[:-1])))
    order = ("z", "xbc", "q", "k", "v", "iq", "ga", "gb", "ik", "dt", "iw")
    for l in range(DEPTH):
        w_bf = w_in[l].astype(BF16)
        cols = [w_bf[:, int(starts[s]):int(starts[s]) + sizes[s]] for s in order]
        used = sum(sizes.values())
        w_in_pad = jnp.concatenate(cols + [jnp.zeros((D_MODEL, IN_WIDTH_PAD - used), BF16)], axis=1)
        lw = (norm1_g[l], w_in_pad, conv_w[l], conv_b[l], dt_bias[l], a_log[l], d_skip[l],
              ssm_norm_g[l], q_norm_g[l], k_norm_g[l], w_branch_a[l].astype(BF16),
              w_branch_b[l].astype(BF16), w_out[l].astype(BF16), norm2_g[l],
              peer_wq[l].astype(BF16),
              peer_keys[l].astype(BF16).reshape(2 * PEER_HEADS, N_KEYS, PEER_HALF),
              peer_u[l].T.astype(BF16), peer_v[l].astype(BF16))
        yp, *rest = _layer(yp, pos_p, lw, None, None, None)
        outs_p.append(rest)
        ys, *rest = _layer(ys, pos_s, lw, state_conv[l], state_ssm, (l, ck_t, cv_t, cik_t, page_table))
        outs_s.append(rest)
    stack = lambda outs, i: jnp.stack([o[i] for o in outs])
    return (yp, ys,
            stack(outs_p, 0), stack(outs_p, 1), stack(outs_p, 2), stack(outs_p, 3), stack(outs_p, 4),
            stack(outs_s, 0), stack(outs_s, 1), stack(outs_s, 2), stack(outs_s, 3), stack(outs_s, 4))
```
